```python
import math, functools
import jax, jax.numpy as jnp
from jax import lax
import numpy as np

D_MODEL = 4096
BATCH = 2
SEQ = 4096
DEPTH = 1
DEC_BATCH = 32
DEC_SEQ = 1
PAST_LEN = 8192
PAGE_SIZE = 128

LRU_WIDTH = D_MODEL // 2
LRU_BLOCKS = 16
LRU_BLOCK = LRU_WIDTH // LRU_BLOCKS
LRU_C = 8.0
CONV_W = 4
N_HEADS = 16
HEAD_DIM = (D_MODEL - LRU_WIDTH) // N_HEADS
ATTN_WIDTH = N_HEADS * HEAD_DIM
IDX_HEADS = 16
IDX_DIM = 64
IDX_SCALE = (IDX_HEADS ** -0.5) * (IDX_DIM ** -0.5)
TOPK_MAX = 256
N_BUCKETS = 32
MAX_DISTANCE = 128
D_FF = 3 * D_MODEL
FFN_CONV_W = 3
Q_BLOCK = 128
LN_EPS = 1e-5
DEEPNORM_ALPHA = (2.0 * DEPTH) ** 0.25
DEEPNORM_BETA = (8.0 * DEPTH) ** -0.25
IN_WIDTH = 2 * LRU_WIDTH + 3 * ATTN_WIDTH + IDX_HEADS * IDX_DIM + IDX_HEADS + IDX_DIM

kernel_name = "hymba_rglru_dsa_convffn_step"


def layer_norm(x, g, b):
    xf = x.astype(jnp.float32)
    mu = jnp.mean(xf, axis=-1, keepdims=True)
    var = jnp.mean(jnp.square(xf - mu), axis=-1, keepdims=True)
    return ((xf - mu) * lax.rsqrt(var + LN_EPS) * g + b).astype(x.dtype)


def rms_norm(x, g):
    xf = x.astype(jnp.float32)
    return (xf * lax.rsqrt(jnp.mean(xf * xf, axis=-1, keepdims=True) + LN_EPS) * g).astype(x.dtype)


def ada_modulation(c, w_ada, b_ada):
    m = (jax.nn.silu(c) @ w_ada + b_ada)[:, None, :]
    return jnp.split(m, 6, axis=-1)


def split_columns(proj):
    sizes = (LRU_WIDTH, LRU_WIDTH, ATTN_WIDTH, ATTN_WIDTH, ATTN_WIDTH, IDX_HEADS * IDX_DIM, IDX_HEADS, IDX_DIM)
    parts, start = [], 0
    for s in sizes:
        parts.append(proj[..., start:start + s])
        start += s
    return parts


def causal_dwconv(xp, w, b):
    width = w.shape[0]
    t = xp.shape[1] - width + 1
    y = b
    for j in range(width):
        y = y + xp[:, j:j + t] * w[j]
    return y


def rg_lru(x, h0, w_a, b_a, w_x, b_x, lam):
    b, t, _ = x.shape
    xb = x.reshape(b, t, LRU_BLOCKS, LRU_BLOCK)
    r = jax.nn.sigmoid(jnp.einsum('btnd,nde->btne', xb, w_a).reshape(b, t, LRU_WIDTH) + b_a)
    i = jax.nn.sigmoid(jnp.einsum('btnd,nde->btne', xb, w_x).reshape(b, t, LRU_WIDTH) + b_x)
    log_a = (-LRU_C * r * jax.nn.softplus(-lam)).astype(jnp.float32)
    a = jnp.exp(log_a)
    u = jnp.sqrt(-jnp.expm1(2.0 * log_a)) * (i * x).astype(jnp.float32)

    def step(h, au):
        a_t, u_t = au
        h = a_t * h + u_t
        return h, h

    h_last, hs = lax.scan(step, h0.astype(jnp.float32), (jnp.swapaxes(a, 0, 1), jnp.swapaxes(u, 0, 1)))
    return jnp.swapaxes(hs, 0, 1).astype(x.dtype), h_last.astype(h0.dtype)


def t5_bucket(dist):
    max_exact = N_BUCKETS // 2
    d = jnp.maximum(dist, 0)
    large = max_exact + (jnp.log(jnp.maximum(d, 1).astype(jnp.float32) / max_exact)
                         / math.log(MAX_DISTANCE / max_exact) * (N_BUCKETS - max_exact)).astype(jnp.int32)
    large = jnp.minimum(large, N_BUCKETS - 1)
    return jnp.where(d < max_exact, d, large)


def indexer_scores(q_idx, w_idx, k_idx):
    s = jax.nn.relu(jnp.einsum('bthd,bsd->bths', q_idx, k_idx))
    return jnp.einsum('bths,bth->bts', s, w_idx)


def gather_rows(rows, idx):
    return jax.vmap(lambda r, i: r[i])(rows, idx)


def gathered_attention(q, k_sel, v_sel, idx, q_pos, rel_bias):
    dist = q_pos[None, :, None] - idx
    bias = jnp.moveaxis(rel_bias[t5_bucket(dist)], -1, 2)
    logits = jnp.einsum('bthd,btkhd->bthk', q, k_sel).astype(jnp.float32) * (HEAD_DIM ** -0.5)
    logits = logits + bias.astype(jnp.float32)
    logits = jnp.where((dist >= 0)[:, :, None, :], logits, -jnp.inf)
    p = jax.nn.softmax(logits, axis=-1).astype(v_sel.dtype)
    return jnp.einsum('bthk,btkhd->bthd', p, v_sel)


def prompt_sparse_attention(q, k, v, q_idx, w_idx, k_idx, rel_bias):
    b, t = q.shape[:2]
    top_k = min(TOPK_MAX, t // 4)
    n_blocks = t // Q_BLOCK
    key_pos = jnp.arange(t)

    def block(j):
        start = j * Q_BLOCK
        qb = lax.dynamic_slice_in_dim(q, start, Q_BLOCK, axis=1)
        qib = lax.dynamic_slice_in_dim(q_idx, start, Q_BLOCK, axis=1)
        wib = lax.dynamic_slice_in_dim(w_idx, start, Q_BLOCK, axis=1)
        q_pos = start + jnp.arange(Q_BLOCK)
        scores = indexer_scores(qib, wib, k_idx)
        scores = jnp.where(key_pos[None, None, :] <= q_pos[None, :, None], scores, -jnp.inf)
        _, idx = lax.top_k(scores, top_k)
        return gathered_attention(qb, gather_rows(k, idx), gather_rows(v, idx), idx, q_pos, rel_bias)

    out = lax.map(block, jnp.arange(n_blocks))
    return jnp.moveaxis(out, 0, 1).reshape(b, t, N_HEADS, HEAD_DIM)


def sample_sparse_attention(q, k, v, q_idx, w_idx, k_idx, cache_k, cache_v, cache_kidx, page_table, rel_bias):
    n_pages = PAST_LEN // PAGE_SIZE
    b, t = q.shape[:2]
    n_keys = PAST_LEN + t
    top_k = min(TOPK_MAX, n_keys // 4)
    kidx_past = cache_kidx[page_table].reshape(b, n_pages * PAGE_SIZE, IDX_DIM)
    kidx_all = jnp.concatenate([kidx_past, k_idx], axis=1)
    q_pos = PAST_LEN + jnp.arange(t)
    scores = indexer_scores(q_idx, w_idx, kidx_all)
    scores = jnp.where(jnp.arange(n_keys)[None, None, :] <= q_pos[None, :, None], scores, -jnp.inf)
    _, idx = lax.top_k(scores, top_k)
    in_past = (idx < PAST_LEN)[..., None, None]
    past_idx = jnp.minimum(idx, PAST_LEN - 1)
    phys = gather_rows(page_table, past_idx // PAGE_SIZE)
    off = past_idx % PAGE_SIZE
    new_idx = jnp.clip(idx - PAST_LEN, 0, t - 1)
    k_sel = jnp.where(in_past, cache_k[phys, off], gather_rows(k, new_idx))
    v_sel = jnp.where(in_past, cache_v[phys, off], gather_rows(v, new_idx))
    return gathered_attention(q, k_sel, v_sel, idx, q_pos, rel_bias)


def decoder_layer(x, c, conv_hist, h0, ffn_hist, attend, p):
    b, t, _ = x.shape
    sh1, sc1, g1, sh2, sc2, g2 = ada_modulation(c, p['w_ada'], p['b_ada'])
    m = x * (1.0 + sc1) + sh1
    xr, gr, q, k, v, qi, wi, ki = split_columns(m @ p['w_in'])
    xr_hist = jnp.concatenate([conv_hist, xr], axis=1)
    xc = causal_dwconv(xr_hist, p['lru_conv_w'], p['lru_conv_b'])
    hs, h_last = rg_lru(xc, h0, p['lru_w_a'], p['lru_b_a'], p['lru_w_x'], p['lru_b_x'], p['lru_lambda'])
    y_lru = hs * jax.nn.gelu(gr)
    q = q.reshape(b, t, N_HEADS, HEAD_DIM)
    k = k.reshape(b, t, N_HEADS, HEAD_DIM)
    v = v.reshape(b, t, N_HEADS, HEAD_DIM)
    qi = qi.reshape(b, t, IDX_HEADS, IDX_DIM)
    wi = wi * IDX_SCALE
    y_att = attend(q, k, v, qi, wi, ki).reshape(b, t, ATTN_WIDTH)
    mix = jnp.concatenate([rms_norm(y_lru, p['lru_out_g']), rms_norm(y_att, p['attn_out_g'])], axis=-1) @ p['w_out']
    x = layer_norm(DEEPNORM_ALPHA * x + g1 * mix, p['ln1_g'], p['ln1_b'])
    m2 = x * (1.0 + sc2) + sh2
    gate, up = jnp.split(m2 @ p['w_ffn_in'], 2, axis=-1)
    g_hist = jnp.concatenate([ffn_hist, gate], axis=1)
    gc = causal_dwconv(g_hist, p['ffn_conv_w'], p['ffn_conv_b'])
    f = (jax.nn.gelu(gc) * up) @ p['w_ffn_out']
    x = layer_norm(DEEPNORM_ALPHA * x + g2 * f, p['ln2_g'], p['ln2_b'])
    new_state = (k, v, ki, h_last, xr_hist[:, -(CONV_W - 1):], g_hist[:, -(FFN_CONV_W - 1):])
    return x, new_state


def setup_inputs(seed: int = 0) -> dict:
    key = jax.random.key(seed)
    ks = jax.random.split(key, 40)

    def nrm(i, shape, scale):
        return scale * jax.random.normal(ks[i], shape, jnp.float32)

    n_pages = PAST_LEN // PAGE_SIZE
    n_phys = (DEC_BATCH * n_pages * 5) // 4
    page_table = jax.random.permutation(ks[0], n_phys)[:DEC_BATCH * n_pages].reshape(DEC_BATCH, n_pages).astype(jnp.int32)
    a = jax.random.uniform(ks[1], (LRU_WIDTH,), jnp.float32, 0.9, 0.999) ** (1.0 / LRU_C)
    lru_lambda = jnp.log(a) - jnp.log1p(-a)
    return {
        'x_prompt': nrm(2, (BATCH, SEQ, D_MODEL), 1.0),
        'x_sample': nrm(3, (DEC_BATCH, DEC_SEQ, D_MODEL), 1.0),
        'cache_k': nrm(4, (n_phys, PAGE_SIZE, N_HEADS, HEAD_DIM), 1.0),
        'cache_v': nrm(5, (n_phys, PAGE_SIZE, N_HEADS, HEAD_DIM), 1.0),
        'cache_kidx': nrm(6, (n_phys, PAGE_SIZE, IDX_DIM), 1.0),
        'page_table': page_table,
        'state_lru_h': nrm(7, (DEC_BATCH, LRU_WIDTH), 0.5),
        'state_lru_conv': nrm(8, (DEC_BATCH, CONV_W - 1, LRU_WIDTH), 1.0),
        'state_ffn_conv': nrm(9, (DEC_BATCH, FFN_CONV_W - 1, D_FF), 1.0),
        'c_prompt': nrm(10, (BATCH, D_MODEL), 1.0),
        'c_sample': nrm(11, (DEC_BATCH, D_MODEL), 1.0),
        'w_ada': nrm(12, (D_MODEL, 6 * D_MODEL), D_MODEL ** -0.5),
        'b_ada': nrm(13, (6 * D_MODEL,), 0.02),
        'w_in': nrm(14, (D_MODEL, IN_WIDTH), D_MODEL ** -0.5),
        'lru_conv_w': nrm(15, (CONV_W, LRU_WIDTH), CONV_W ** -0.5),
        'lru_conv_b': nrm(16, (LRU_WIDTH,), 0.02),
        'lru_w_a': nrm(17, (LRU_BLOCKS, LRU_BLOCK, LRU_BLOCK), LRU_BLOCK ** -0.5),
        'lru_b_a': nrm(18, (LRU_WIDTH,), 0.02),
        'lru_w_x': nrm(19, (LRU_BLOCKS, LRU_BLOCK, LRU_BLOCK), LRU_BLOCK ** -0.5),
        'lru_b_x': nrm(20, (LRU_WIDTH,), 0.02),
        'lru_lambda': lru_lambda,
        'attn_rel_bias': nrm(21, (N_BUCKETS, N_HEADS), 0.5),
        'lru_out_g': 1.0 + nrm(22, (LRU_WIDTH,), 0.02),
        'attn_out_g': 1.0 + nrm(23, (ATTN_WIDTH,), 0.02),
        'w_out': nrm(24, (LRU_WIDTH + ATTN_WIDTH, D_MODEL), DEEPNORM_BETA * (LRU_WIDTH + ATTN_WIDTH) ** -0.5),
        'ln1_g': 1.0 + nrm(25, (D_MODEL,), 0.02),
        'ln1_b': nrm(26, (D_MODEL,), 0.02),
        'w_ffn_in': nrm(27, (D_MODEL, 2 * D_FF), D_MODEL ** -0.5),
        'ffn_conv_w': nrm(28, (FFN_CONV_W, D_FF), FFN_CONV_W ** -0.5),
        'ffn_conv_b': nrm(29, (D_FF,), 0.02),
        'w_ffn_out': nrm(30, (D_FF, D_MODEL), DEEPNORM_BETA * D_FF ** -0.5),
        'ln2_g': 1.0 + nrm(31, (D_MODEL,), 0.02),
        'ln2_b': nrm(32, (D_MODEL,), 0.02),
    }


def reference(x_prompt, x_sample, cache_k, cache_v, cache_kidx, page_table, state_lru_h, state_lru_conv,
              state_ffn_conv, c_prompt, c_sample, w_ada, b_ada, w_in, lru_conv_w, lru_conv_b, lru_w_a, lru_b_a,
              lru_w_x, lru_b_x, lru_lambda, attn_rel_bias, lru_out_g, attn_out_g, w_out, ln1_g, ln1_b, w_ffn_in,
              ffn_conv_w, ffn_conv_b, w_ffn_out, ln2_g, ln2_b):
    p = dict(w_ada=w_ada, b_ada=b_ada, w_in=w_in, lru_conv_w=lru_conv_w, lru_conv_b=lru_conv_b,
             lru_w_a=lru_w_a, lru_b_a=lru_b_a, lru_w_x=lru_w_x, lru_b_x=lru_b_x, lru_lambda=lru_lambda,
             lru_out_g=lru_out_g, attn_out_g=attn_out_g, w_out=w_out, ln1_g=ln1_g, ln1_b=ln1_b,
             w_ffn_in=w_ffn_in, ffn_conv_w=ffn_conv_w, ffn_conv_b=ffn_conv_b, w_ffn_out=w_ffn_out,
             ln2_g=ln2_g, ln2_b=ln2_b)
    attend_prompt = functools.partial(prompt_sparse_attention, rel_bias=attn_rel_bias)
    attend_sample = functools.partial(sample_sparse_attention, cache_k=cache_k, cache_v=cache_v,
                                      cache_kidx=cache_kidx, page_table=page_table, rel_bias=attn_rel_bias)
    b = x_prompt.shape[0]
    zero_conv = jnp.zeros((b, CONV_W - 1, LRU_WIDTH), x_prompt.dtype)
    zero_h = jnp.zeros((b, LRU_WIDTH), x_prompt.dtype)
    zero_ffn = jnp.zeros((b, FFN_CONV_W - 1, D_FF), x_prompt.dtype)
    y_prompt, y_sample = x_prompt, x_sample
    for _ in range(DEPTH):
        y_prompt, st_p = decoder_layer(y_prompt, c_prompt, zero_conv, zero_h, zero_ffn, attend_prompt, p)
        y_sample, st_s = decoder_layer(y_sample, c_sample, state_lru_conv, state_lru_h, state_ffn_conv, attend_sample, p)
    k_p, v_p, kidx_p, h_p, conv_p, ffn_p = st_p
    k_s, v_s, kidx_s, h_s, conv_s, ffn_s = st_s
    return (y_prompt, y_sample, k_p, v_p, kidx_p, h_p, conv_p, ffn_p, k_s, v_s, kidx_s, h_s, conv_s, ffn_s)
```

```python
import functools
import math

import numpy as np
import jax
import jax.numpy as jnp
from jax import lax
from jax.experimental import pallas as pl
from jax.experimental.pallas import tpu as pltpu

F32 = jnp.float32
BF16 = jnp.bfloat16
I32 = jnp.int32

LRU_C = 8.0
TOPK_MAX = 256
MAX_DISTANCE = 128
LN_EPS = 1e-5
DEPTH = 1
DEEPNORM_ALPHA = (2.0 * DEPTH) ** 0.25
LANES = 128
VMEM_LIMIT = 56 * 1024 * 1024
INT_MIN = -(2 ** 31)
NEG_INF = float("-inf")


def _cp(sem, vmem=VMEM_LIMIT):
    return pltpu.CompilerParams(dimension_semantics=sem, vmem_limit_bytes=vmem)


def _sds(shape, dtype):
    return jax.ShapeDtypeStruct(shape, dtype)


def _split_bf16(x):
    hi = x.astype(BF16)
    lo = (x - hi.astype(F32)).astype(BF16)
    return hi, lo


def _dot(a, b):
    return jnp.dot(a, b, preferred_element_type=F32)


def _dot_nt(a, b):
    return lax.dot_general(a, b, (((1,), (1,)), ((), ())), preferred_element_type=F32)


def _dot3(a, b):
    ah, al = _split_bf16(a)
    bh, bl = _split_bf16(b)
    return _dot(ah, bh) + (_dot(al, bh) + _dot(ah, bl))


def _sortable(x):
    b = pltpu.bitcast(x, I32)
    return b ^ ((b >> 31) & 0x7FFFFFFF)


class _Stream:
    def __init__(self, G, R, tm, per_row, mod_row0):
        self.G, self.R, self.tm, self.per_row, self.mod_row0 = G, R, tm, per_row, mod_row0
        self.M = G * R
        self.nr = R // tm

    def mod_operand(self, mod2):
        if self.per_row:
            return mod2
        mp, n6 = mod2.shape
        return mod2.reshape(mp, 6, 1, n6 // 6)

    def mod_spec(self, which, width, d_model, col=lambda *ids: 0):
        if self.per_row:
            nb = d_model // width
            return pl.BlockSpec((self.tm, width), lambda g, i, *r: (i, which * nb + col(g, i, *r)))
        r0 = self.mod_row0
        return pl.BlockSpec((None, None, 1, width), lambda g, i, *r: (r0 + g, which, 0, col(g, i, *r)))


def _ada_kernel(c_ref, w_ref, b_ref, o_ref):
    c = c_ref[...]
    a = c * jax.nn.sigmoid(c)
    o_ref[...] = _dot3(a, w_ref[...]) + b_ref[...]


def _ada(c_all, w_ada, b_ada):
    mp, d = c_all.shape
    n = w_ada.shape[1]
    tn = 512
    return pl.pallas_call(
        _ada_kernel,
        out_shape=_sds((mp, n), F32),
        grid=(n // tn,),
        in_specs=[pl.BlockSpec((mp, d), lambda j: (0, 0)),
                  pl.BlockSpec((d, tn), lambda j: (0, j)),
                  pl.BlockSpec((1, tn), lambda j: (0, j))],
        out_specs=pl.BlockSpec((mp, tn), lambda j: (0, j)),
        compiler_params=_cp(("arbitrary",)),
        name="ada_mod",
    )(c_all, w_ada, b_ada.reshape(1, n))


def _mod_kernel(x_ref, sh_ref, sc_ref, o_ref):
    o_ref[...] = (x_ref[...] * (1.0 + sc_ref[...]) + sh_ref[...]).astype(o_ref.dtype)


def _modulate(st, x3, mod2, sh_which, sc_which):
    d = x3.shape[-1]
    tm = min(st.tm, 512)
    st2 = _Stream(st.G, st.R, tm, st.per_row, st.mod_row0)
    modop = st2.mod_operand(mod2)
    return pl.pallas_call(
        _mod_kernel,
        out_shape=_sds((st.G, st.R, d), BF16),
        grid=(st.G, st.R // tm),
        in_specs=[pl.BlockSpec((None, tm, d), lambda g, i: (g, i, 0)),
                  st2.mod_spec(sh_which, d, d), st2.mod_spec(sc_which, d, d)],
        out_specs=pl.BlockSpec((None, tm, d), lambda g, i: (g, i, 0)),
        compiler_params=_cp(("arbitrary", "arbitrary")),
        name="modulate",
    )(x3, modop, modop)


def _mm_kernel(a_ref, b_ref, *o_refs):
    r = _dot(a_ref[...], b_ref[...])
    for o in o_refs:
        o[...] = r.astype(o.dtype)


def _matmul(a, b, col0, n, out_dtypes, tm, tn=1024):
    m, k = a.shape
    tm = min(tm, m)
    tn = min(tn, n)
    assert m % tm == 0 and n % tn == 0 and col0 % tn == 0
    cb = col0 // tn
    outs = pl.pallas_call(
        _mm_kernel,
        out_shape=[_sds((m, n), dt) for dt in out_dtypes],
        grid=(m // tm, n // tn),
        in_specs=[pl.BlockSpec((tm, k), lambda i, j: (i, 0)),
                  pl.BlockSpec((k, tn), lambda i, j: (0, cb + j))],
        out_specs=[pl.BlockSpec((tm, tn), lambda i, j: (i, j)) for _ in out_dtypes],
        compiler_params=_cp(("arbitrary", "arbitrary")),
        name="matmul",
    )(a, b)
    return outs


def _mm2_kernel(a1_ref, a2_ref, b_ref, o_ref):
    k1 = a1_ref.shape[1]
    o_ref[...] = _dot(a1_ref[...], b_ref[0:k1, :]) + _dot(a2_ref[...], b_ref[k1:, :])


def _matmul_cat(a1, a2, b, tm, tn=1024):
    m, k1 = a1.shape
    k2 = a2.shape[1]
    n = b.shape[1]
    tm = min(tm, m)
    return pl.pallas_call(
        _mm2_kernel,
        out_shape=_sds((m, n), F32),
        grid=(m // tm, n // tn),
        in_specs=[pl.BlockSpec((tm, k1), lambda i, j: (i, 0)),
                  pl.BlockSpec((tm, k2), lambda i, j: (i, 0)),
                  pl.BlockSpec((k1 + k2, tn), lambda i, j: (0, j))],
        out_specs=pl.BlockSpec((tm, tn), lambda i, j: (i, j)),
        compiler_params=_cp(("arbitrary", "arbitrary")),
        name="matmul_cat",
    )(a1, a2, b)


def _mmk_kernel(a_ref, b_ref, o_ref):
    part = _dot(a_ref[...], b_ref[...])

    @pl.when(pl.program_id(2) == 0)
    def _():
        o_ref[...] = part

    @pl.when(pl.program_id(2) > 0)
    def _():
        o_ref[...] += part


def _matmul_ktiled(a, b, tm, tn, tk):
    m, k = a.shape
    n = b.shape[1]
    tm = min(tm, m)
    return pl.pallas_call(
        _mmk_kernel,
        out_shape=_sds((m, n), F32),
        grid=(m // tm, n // tn, k // tk),
        in_specs=[pl.BlockSpec((tm, tk), lambda i, j, kk: (i, kk)),
                  pl.BlockSpec((tk, tn), lambda i, j, kk: (kk, j))],
        out_specs=pl.BlockSpec((tm, tn), lambda i, j, kk: (i, j)),
        compiler_params=_cp(("arbitrary", "arbitrary", "arbitrary")),
        name="matmul_ktiled",
    )(a, b)


def _idxproj_kernel(x_ref, sh_ref, sc_ref, wh_ref, wl_ref, q_ref, wk_ref, acc_ref, *, nq):
    kk = pl.program_id(2)
    m = x_ref[...] * (1.0 + sc_ref[...]) + sh_ref[...]
    mh, ml = _split_bf16(m)
    wh = wh_ref[...]
    part = _dot(mh, wh) + (_dot(ml, wh) + _dot(mh, wl_ref[...]))

    @pl.when(kk == 0)
    def _():
        acc_ref[...] = part

    @pl.when(kk > 0)
    def _():
        acc_ref[...] += part

    @pl.when(kk == pl.num_programs(2) - 1)
    def _():
        q_ref[...] = acc_ref[:, 0:nq]
        wk_ref[...] = acc_ref[:, nq:]


def _idx_project(st, x3, mod2, w_hi, w_lo, nq):
    d = x3.shape[-1]
    nw = w_hi.shape[1]
    tm = min(st.tm, 512)
    tk = 1024
    st2 = _Stream(st.G, st.R, tm, st.per_row, st.mod_row0)
    modop = st2.mod_operand(mod2)
    kcol = lambda g, i, kk: kk
    q, wk = pl.pallas_call(
        functools.partial(_idxproj_kernel, nq=nq),
        out_shape=[_sds((st.G, st.R, nq), F32), _sds((st.G, st.R, nw - nq), F32)],
        grid=(st.G, st.R // tm, d // tk),
        in_specs=[pl.BlockSpec((None, tm, tk), lambda g, i, kk: (g, i, kk)),
                  st2.mod_spec(0, tk, d, kcol), st2.mod_spec(1, tk, d, kcol),
                  pl.BlockSpec((tk, nw), lambda g, i, kk: (kk, 0)),
                  pl.BlockSpec((tk, nw), lambda g, i, kk: (kk, 0))],
        out_specs=[pl.BlockSpec((None, tm, nq), lambda g, i, kk: (g, i, 0)),
                   pl.BlockSpec((None, tm, nw - nq), lambda g, i, kk: (g, i, 0))],
        scratch_shapes=[pltpu.VMEM((tm, nw), F32)],
        compiler_params=_cp(("arbitrary", "arbitrary", "arbitrary")),
        name="idx_project",
    )(x3, modop, modop, w_hi, w_lo)
    return q, wk


def _softplus(z):
    return jnp.maximum(z, 0.0) + jnp.log1p(jnp.exp(-jnp.abs(z)))


def _expm1(x):
    poly = x * (1.0 + x * (1.0 / 2) * (1.0 + x * (1.0 / 3) * (1.0 + x * (1.0 / 4) * (1.0 + x * (1.0 / 5) * (1.0 + x * (1.0 / 6))))))
    return jnp.where(jnp.abs(x) < 0.25, poly, jnp.exp(x) - 1.0)


def _lru_gates(xc, wax, ba, bx, lam):
    ri = _dot(xc.astype(BF16), wax)
    r = jax.nn.sigmoid(ri[:, :LANES] + ba)
    ig = jax.nn.sigmoid(ri[:, LANES:] + bx)
    log_a = (-LRU_C * r) * _softplus(-lam)
    a = jnp.exp(log_a)
    u = jnp.sqrt(-_expm1(2.0 * log_a)) * (ig * xc)
    return a, u


def _lru_prompt_kernel(xr_ref, gr_ref, hist_ref, h0_ref, cw_ref, cb_ref, wax_ref, ba_ref, bx_ref, lam_ref, g_ref,
                       y_ref, hlast_ref, xbuf, hcar, ybuf):
    t = pl.program_id(1)
    tc, w = xr_ref.shape
    nb = w // LANES

    @pl.when(t == 0)
    def _():
        xbuf[0:8, :] = hist_ref[...]
        hcar[...] = h0_ref[...]

    @pl.when(t > 0)
    def _():
        xbuf[0:8, :] = xbuf[tc:tc + 8, :]

    xbuf[8:8 + tc, :] = xr_ref[...]
    rows = lax.broadcasted_iota(I32, (tc, LANES), 0)
    ssq = jnp.zeros((tc, 1), F32)
    for n in range(nb):
        sl = slice(n * LANES, (n + 1) * LANES)
        xc = cb_ref[:, sl]
        for j in range(4):
            xc = xc + xbuf[5 + j:5 + j + tc, sl] * cw_ref[j:j + 1, sl]
        a, u = _lru_gates(xc, wax_ref[n], ba_ref[:, sl], bx_ref[:, sl], lam_ref[:, sl])
        s = 1
        while s < tc:
            keep = rows >= s
            u = jnp.where(keep, a * pltpu.roll(u, s, axis=0) + u, u)
            a = jnp.where(keep, a * pltpu.roll(a, s, axis=0), a)
            s *= 2
        h = a * hcar[:, sl] + u
        hcar[:, sl] = h[tc - 1:tc, :]
        y = h * jax.nn.gelu(gr_ref[:, sl])
        ybuf[:, sl] = y
        ssq = ssq + jnp.sum(y * y, axis=1, keepdims=True)
    scale = lax.rsqrt(ssq * (1.0 / w) + LN_EPS)
    y_ref[...] = (ybuf[...] * scale * g_ref[...]).astype(y_ref.dtype)
    hlast_ref[...] = hcar[...]


def _lru_prompt(xg, B, T, hist8, h0, p):
    w = xg.shape[1] // 2
    tc = min(256, T)
    nt = T // tc
    vec = lambda: pl.BlockSpec((1, w), lambda b, t: (0, 0))
    y, hl = pl.pallas_call(
        _lru_prompt_kernel,
        out_shape=[_sds((B * T, w), BF16), _sds((B, 1, w), F32)],
        grid=(B, nt),
        in_specs=[pl.BlockSpec((tc, w), lambda b, t: (b * nt + t, 0)),
                  pl.BlockSpec((tc, w), lambda b, t: (b * nt + t, 1)),
                  pl.BlockSpec((None, 8, w), lambda b, t: (b, 0, 0)),
                  pl.BlockSpec((None, 1, w), lambda b, t: (b, 0, 0)),
                  pl.BlockSpec((4, w), lambda b, t: (0, 0)), vec(),
                  pl.BlockSpec(p["wax"].shape, lambda b, t: (0, 0, 0)),
                  vec(), vec(), vec(), vec()],
        out_specs=[pl.BlockSpec((tc, w), lambda b, t: (b * nt + t, 0)),
                   pl.BlockSpec((None, 1, w), lambda b, t: (b, 0, 0))],
        scratch_shapes=[pltpu.VMEM((tc + 8, w), F32), pltpu.VMEM((1, w), F32), pltpu.VMEM((tc, w), F32)],
        compiler_params=_cp(("arbitrary", "arbitrary")),
        name="rglru_prompt",
    )(xg, xg, hist8, h0, p["cw"], p["cb"], p["wax"], p["ba"], p["bx"], p["lam"], p["g"])
    return y, hl.reshape(B, w)


def _lru_sample_kernel(xr_ref, gr_ref, hist_ref, h0_ref, cw_ref, cb_ref, wax_ref, ba_ref, bx_ref, lam_ref, g_ref,
                       y_ref, h_ref, ybuf):
    r, w = xr_ref.shape
    nb = w // LANES
    ssq = jnp.zeros((r, 1), F32)
    for n in range(nb):
        sl = slice(n * LANES, (n + 1) * LANES)
        xc = cb_ref[:, sl]
        for j in range(3):
            xc = xc + hist_ref[j, :, sl] * cw_ref[j:j + 1, sl]
        xc = xc + xr_ref[:, sl] * cw_ref[3:4, sl]
        a, u = _lru_gates(xc, wax_ref[n], ba_ref[:, sl], bx_ref[:, sl], lam_ref[:, sl])
        h = a * h0_ref[:, sl] + u
        h_ref[:, sl] = h
        y = h * jax.nn.gelu(gr_ref[:, sl])
        ybuf[:, sl] = y
        ssq = ssq + jnp.sum(y * y, axis=1, keepdims=True)
    scale = lax.rsqrt(ssq * (1.0 / w) + LN_EPS)
    y_ref[...] = (ybuf[...] * scale * g_ref[...]).astype(y_ref.dtype)


def _lru_sample(xg, hist_t, h0, p):
    s = xg.shape[0]
    w = xg.shape[1] // 2
    vec = lambda: pl.BlockSpec((1, w), lambda i: (0, 0))
    return pl.pallas_call(
        _lru_sample_kernel,
        out_shape=[_sds((s, w), BF16), _sds((s, w), F32)],
        grid=(1,),
        in_specs=[pl.BlockSpec((s, w), lambda i: (0, 0)), pl.BlockSpec((s, w), lambda i: (0, 1)),
                  pl.BlockSpec((3, s, w), lambda i: (0, 0, 0)), pl.BlockSpec((s, w), lambda i: (0, 0)),
                  pl.BlockSpec((4, w), lambda i: (0, 0)), vec(),
                  pl.BlockSpec(p["wax"].shape, lambda i: (0, 0, 0)),
                  vec(), vec(), vec(), vec()],
        out_specs=[pl.BlockSpec((s, w), lambda i: (0, 0)), pl.BlockSpec((s, w), lambda i: (0, 0))],
        scratch_shapes=[pltpu.VMEM((s, w), F32)],
        compiler_params=_cp(("arbitrary",)),
        name="rglru_sample",
    )(xg, xg, hist_t, h0, p["cw"], p["cb"], p["wax"], p["ba"], p["bx"], p["lam"], p["g"])


def _kth_largest(count_ge, kk, rows):
    def body(p, thr):
        bit = jnp.left_shift(jnp.int32(1), 31 - p)
        trial = thr + bit
        return jnp.where(count_ge(trial) >= kk, trial, thr)
    return lax.fori_loop(0, 32, body, jnp.full((rows, 1), INT_MIN, I32))


def _pidx_kernel(q_ref, wkq_ref, wkall_ref, o_ref, kcat, qcat, wb, keys, *, n_heads, idim, kk, kc):
    i = pl.program_id(1)
    tq = q_ref.shape[0]
    t_all = wkall_ref.shape[0]
    lane = lax.broadcasted_iota(I32, (tq, LANES), 1)

    @pl.when(i == 0)
    def _():
        k = wkall_ref[...]
        kh = k.astype(BF16).astype(F32)
        kl = k - kh
        left = lax.broadcasted_iota(I32, k.shape, 1) < idim
        kcat[:, 0:LANES] = jnp.where(left, kh, pltpu.roll(kl, idim, axis=1)).astype(BF16)
        kcat[:, LANES:] = jnp.where(left, kh, 0.0).astype(BF16)

    for pr in range(n_heads // 2):
        v = q_ref[:, pr * LANES:(pr + 1) * LANES]
        vh = v.astype(BF16).astype(F32)
        vl = v - vh
        vh_r = pltpu.roll(vh, idim, axis=1)
        vl_r = pltpu.roll(vl, idim, axis=1)
        first = lane < idim
        qcat[2 * pr, :, 0:LANES] = jnp.where(first, vh, vh_r).astype(BF16)
        qcat[2 * pr, :, LANES:] = jnp.where(first, vl, 0.0).astype(BF16)
        qcat[2 * pr + 1, :, 0:LANES] = jnp.where(first, vh_r, vh).astype(BF16)
        qcat[2 * pr + 1, :, LANES:] = jnp.where(first, vl_r, 0.0).astype(BF16)
    scale = (n_heads ** -0.5) * (idim ** -0.5)
    wq = wkq_ref[...] * scale
    for h in range(n_heads):
        wb[h] = jnp.broadcast_to(wq[:, idim + h:idim + h + 1], (tq, LANES))

    nch = (i * tq + tq + kc - 1) // kc
    qpos = i * tq + lax.broadcasted_iota(I32, (tq, kc), 0)
    cols = lax.broadcasted_iota(I32, (tq, kc), 1)

    def score_chunk(c, carry):
        k_c = kcat[pl.ds(pl.multiple_of(c * kc, kc), kc), :]
        acc = jnp.zeros((tq, kc), F32)
        for h in range(n_heads):
            x = _dot_nt(qcat[h], k_c)
            acc = acc + jnp.maximum(x, 0.0) * jnp.tile(wb[h], (1, kc // LANES))
        key = jnp.where(c * kc + cols <= qpos, _sortable(acc + 0.0), INT_MIN)
        keys[:, pl.ds(pl.multiple_of(c * kc, kc), kc)] = key
        return carry

    lax.fori_loop(0, nch, score_chunk, 0)

    def count_where(pred):
        def body(c, cnt):
            m = pred(keys[:, pl.ds(pl.multiple_of(c * kc, kc), kc)]).astype(I32)
            for s in range(kc // LANES):
                cnt = cnt + m[:, s * LANES:(s + 1) * LANES]
            return cnt
        cnt = lax.fori_loop(0, nch, body, jnp.zeros((tq, LANES), I32))
        return jnp.sum(cnt, axis=1, keepdims=True)

    thr = _kth_largest(lambda trial: count_where(lambda kv: kv >= trial), kk, tq)
    n_ge = count_where(lambda kv: (kv >= thr) & (kv > INT_MIN))
    tie = jnp.max(n_ge) > kk

    o_ref[...] = jnp.full(o_ref.shape, NEG_INF, F32)

    @pl.when(jnp.logical_not(tie))
    def _():
        def body(c, carry):
            sl = pl.ds(pl.multiple_of(c * kc, kc), kc)
            kv = keys[:, sl]
            o_ref[:, sl] = jnp.where((kv >= thr) & (kv > INT_MIN), 0.0, NEG_INF)
            return carry
        lax.fori_loop(0, nch, body, 0)

    @pl.when(tie)
    def _():
        n_gt = count_where(lambda kv: kv > thr)
        need = (kk - n_gt).astype(F32)
        tri = (lax.broadcasted_iota(I32, (kc, kc), 0) < lax.broadcasted_iota(I32, (kc, kc), 1)).astype(BF16)

        def body(c, seen):
            sl = pl.ds(pl.multiple_of(c * kc, kc), kc)
            kv = keys[:, sl]
            eq = (kv == thr) & (kv > INT_MIN)
            eqf = jnp.where(eq, 1.0, 0.0)
            before = seen + _dot(eqf.astype(BF16), tri)
            sel = (kv > thr) | (eq & (before < need))
            o_ref[:, sl] = jnp.where(sel, 0.0, NEG_INF)
            return seen + jnp.sum(eqf, axis=1, keepdims=True)
        lax.fori_loop(0, nch, body, jnp.zeros((tq, 1), F32))


def _prompt_index(qi, wk, B, T, n_heads, idim, kk):
    tq = 128
    kc = min(512, T)
    nq = T // tq
    return pl.pallas_call(
        functools.partial(_pidx_kernel, n_heads=n_heads, idim=idim, kk=kk, kc=kc),
        out_shape=_sds((B * T, T), F32),
        grid=(B, nq),
        in_specs=[pl.BlockSpec((tq, n_heads * idim), lambda b, i: (b * nq + i, 0)),
                  pl.BlockSpec((tq, LANES), lambda b, i: (b * nq + i, 0)),
                  pl.BlockSpec((T, LANES), lambda b, i: (b, 0))],
        out_specs=pl.BlockSpec((tq, T), lambda b, i: (b * nq + i, 0)),
        scratch_shapes=[pltpu.VMEM((T, 2 * LANES), BF16), pltpu.VMEM((n_heads, tq, 2 * LANES), BF16),
                        pltpu.VMEM((n_heads, tq, LANES), F32), pltpu.VMEM((tq, T), I32)],
        compiler_params=_cp(("arbitrary", "arbitrary")),
        name="prompt_index",
    )(qi, wk, wk)


def _bucket_np(d):
    n_buckets = 32
    max_exact = n_buckets // 2
    d = np.maximum(d, 0)
    large = max_exact + (np.log(np.maximum(d, 1).astype(np.float32) / np.float32(max_exact))
                         / np.float32(math.log(MAX_DISTANCE / max_exact))
                         * np.float32(n_buckets - max_exact)).astype(np.int32)
    large = np.minimum(large, n_buckets - 1)
    return np.where(d < max_exact, d, large).astype(np.int32)


def _bucket_jnp(d, n_buckets):
    max_exact = n_buckets // 2
    d = jnp.maximum(d, 0)
    large = max_exact + (jnp.log(jnp.maximum(d, 1).astype(F32) / max_exact)
                         / math.log(MAX_DISTANCE / max_exact) * (n_buckets - max_exact)).astype(I32)
    large = jnp.minimum(large, n_buckets - 1)
    return jnp.where(d < max_exact, d, large)


def _pattn_kernel(rb_ref, q_ref, k_ref, v_ref, mask_ref, bkt_ref, o_ref, tbl, m_s, l_s, acc_s, *, hp, kc, n_far):
    hg = pl.program_id(1)
    i = pl.program_id(2)
    tq = q_ref.shape[0]
    dh = q_ref.shape[1] // hp
    scale = dh ** -0.5
    n_buckets = rb_ref.shape[0]

    @pl.when(i == 0)
    def _():
        bkt = bkt_ref[...]
        for h in range(hp):
            acc = jnp.zeros(bkt.shape, F32)
            for b in range(n_buckets):
                acc = jnp.where(bkt == b, rb_ref[b, hg * hp + h], acc)
            tbl[h] = acc

    for h in range(hp):
        m_s[h] = jnp.full((tq, 1), NEG_INF, F32)
        l_s[h] = jnp.zeros((tq, 1), F32)
        acc_s[h] = jnp.zeros((tq, dh), F32)

    nch = (i * tq + tq + kc - 1) // kc
    c_near = jnp.maximum((i * tq - tq) // kc, 0)

    def chunk(c, bias_of):
        sl = pl.ds(pl.multiple_of(c * kc, kc), kc)
        msk = mask_ref[:, sl]
        for h in range(hp):
            hs = slice(h * dh, (h + 1) * dh)
            s = _dot_nt(q_ref[:, hs], k_ref[sl, hs]) * scale + bias_of(h) + msk
            m_old = m_s[h]
            m_new = jnp.maximum(m_old, jnp.max(s, axis=1, keepdims=True))
            m_safe = jnp.where(m_new == NEG_INF, 0.0, m_new)
            alpha = jnp.exp(m_old - m_safe)
            p = jnp.exp(s - m_safe)
            l_s[h] = alpha * l_s[h] + jnp.sum(p, axis=1, keepdims=True)
            acc_s[h] = alpha * acc_s[h] + _dot(p.astype(BF16), v_ref[sl, hs])
            m_s[h] = m_new

    def far(c, carry):
        chunk(c, lambda h: rb_ref[n_far, hg * hp + h])
        return carry

    def near(c, carry):
        off = pl.multiple_of(kc - (i * tq - c * kc), LANES)
        chunk(c, lambda h: tbl[h, :, pl.ds(off, kc)])
        return carry

    lax.fori_loop(0, c_near, far, 0)
    lax.fori_loop(c_near, nch, near, 0)
    for h in range(hp):
        o_ref[:, h * dh:(h + 1) * dh] = acc_s[h] / l_s[h]


def _prompt_attention(q, k, v, mask, rel_bias, B, T, n_heads, dh):
    tq = 128
    kc = min(512, T)
    hp = 4
    nq = T // tq
    n_buckets = rel_bias.shape[0]
    r = np.arange(tq)[:, None]
    x = np.arange(2 * kc)[None, :]
    dist = r + kc - x
    bkt = np.where(dist >= 0, _bucket_np(dist), -1).astype(np.int32)
    far_d = kc + tq
    assert _bucket_np(np.array([tq + 1]))[0] == n_buckets - 1 and far_d > tq
    grid_spec = pltpu.PrefetchScalarGridSpec(
        num_scalar_prefetch=0,
        grid=(B, n_heads // hp, nq),
        in_specs=[pl.BlockSpec(memory_space=pltpu.SMEM),
                  pl.BlockSpec((tq, hp * dh), lambda b, g, i: (b * nq + i, g)),
                  pl.BlockSpec((T, hp * dh), lambda b, g, i: (b, g)),
                  pl.BlockSpec((T, hp * dh), lambda b, g, i: (b, g)),
                  pl.BlockSpec((tq, T), lambda b, g, i: (b * nq + i, 0)),
                  pl.BlockSpec((tq, 2 * kc), lambda b, g, i: (0, 0))],
        out_specs=pl.BlockSpec((tq, hp * dh), lambda b, g, i: (b * nq + i, g)),
        scratch_shapes=[pltpu.VMEM((hp, tq, 2 * kc), F32), pltpu.VMEM((hp, tq, 1), F32),
                        pltpu.VMEM((hp, tq, 1), F32), pltpu.VMEM((hp, tq, dh), F32)],
    )
    return pl.pallas_call(
        functools.partial(_pattn_kernel, hp=hp, kc=kc, n_far=n_buckets - 1),
        out_shape=_sds((B * T, n_heads * dh), F32),
        grid_spec=grid_spec,
        compiler_params=_cp(("arbitrary", "arbitrary", "arbitrary")),
        name="prompt_attention",
    )(rel_bias, q, k, v, mask, jnp.asarray(bkt))


def _rms_kernel(x_ref, g_ref, o_ref):
    x = x_ref[...]
    ms = jnp.mean(x * x, axis=-1, keepdims=True)
    o_ref[...] = (x * lax.rsqrt(ms + LN_EPS) * g_ref[...]).astype(o_ref.dtype)


def _rms_norm(x, g, tm):
    m, w = x.shape
    tm = min(tm, m)
    return pl.pallas_call(
        _rms_kernel,
        out_shape=_sds((m, w), BF16),
        grid=(m // tm,),
        in_specs=[pl.BlockSpec((tm, w), lambda i: (i, 0)), pl.BlockSpec((1, w), lambda i: (0, 0))],
        out_specs=pl.BlockSpec((tm, w), lambda i: (i, 0)),
        compiler_params=_cp(("arbitrary",)),
        name="rms_norm",
    )(x, g)


def _layer_norm(z, g, b):
    mu = jnp.mean(z, axis=-1, keepdims=True)
    zc = z - mu
    var = jnp.mean(zc * zc, axis=-1, keepdims=True)
    return zc * lax.rsqrt(var + LN_EPS) * g + b


def _ln_mod_kernel(x_ref, f_ref, gate_ref, sh_ref, sc_ref, g_ref, b_ref, x1_ref, m_ref):
    z = DEEPNORM_ALPHA * x_ref[...] + gate_ref[...] * f_ref[...]
    x1 = _layer_norm(z, g_ref[...], b_ref[...])
    x1_ref[...] = x1
    m_ref[...] = (x1 * (1.0 + sc_ref[...]) + sh_ref[...]).astype(m_ref.dtype)


def _ln_kernel(x_ref, f_ref, gate_ref, g_ref, b_ref, y_ref):
    z = DEEPNORM_ALPHA * x_ref[...] + gate_ref[...] * f_ref[...]
    y_ref[...] = _layer_norm(z, g_ref[...], b_ref[...])


def _residual_ln(st, x3, f3, mod2, gate_which, ln_g, ln_b, mod_next=None):
    d = x3.shape[-1]
    tm = min(st.tm, 256)
    st2 = _Stream(st.G, st.R, tm, st.per_row, st.mod_row0)
    modop = st2.mod_operand(mod2)
    row = pl.BlockSpec((None, tm, d), lambda g, i: (g, i, 0))
    vec = pl.BlockSpec((1, d), lambda g, i: (0, 0))
    if mod_next is None:
        return pl.pallas_call(
            _ln_kernel,
            out_shape=_sds(x3.shape, F32),
            grid=(st.G, st.R // tm),
            in_specs=[row, row, st2.mod_spec(gate_which, d, d), vec, vec],
            out_specs=row,
            compiler_params=_cp(("arbitrary", "arbitrary")),
            name="residual_ln",
        )(x3, f3, modop, ln_g, ln_b)
    return pl.pallas_call(
        _ln_mod_kernel,
        out_shape=[_sds(x3.shape, F32), _sds(x3.shape, BF16)],
        grid=(st.G, st.R // tm),
        in_specs=[row, row, st2.mod_spec(gate_which, d, d), st2.mod_spec(mod_next[0], d, d),
                  st2.mod_spec(mod_next[1], d, d), vec, vec],
        out_specs=[row, row],
        compiler_params=_cp(("arbitrary", "arbitrary")),
        name="residual_ln_mod",
    )(x3, f3, modop, modop, modop, ln_g, ln_b)


def _ffn_in_seq_kernel(a_ref, wg_ref, wu_ref, hist_ref, cw_ref, cb_ref, h_ref, tail_ref, carry, *, rows_per_seq):
    i = pl.program_id(1)
    tm = a_ref.shape[0]
    tiles_per_seq = rows_per_seq // tm
    a = a_ref[...]
    gate = _dot(a, wg_ref[...])
    up = _dot(a, wu_ref[...])

    @pl.when(i % tiles_per_seq == 0)
    def _():
        carry[...] = hist_ref[...]

    rows = lax.broadcasted_iota(I32, gate.shape, 0)
    prev = carry[...]
    g1 = jnp.where(rows >= 1, pltpu.roll(gate, 1, axis=0), prev[7:8, :])
    g2 = jnp.where(rows >= 2, pltpu.roll(gate, 2, axis=0), jnp.where(rows == 1, prev[7:8, :], prev[6:7, :]))
    gc = cb_ref[...] + g2 * cw_ref[0:1, :] + g1 * cw_ref[1:2, :] + gate * cw_ref[2:3, :]
    h_ref[...] = (jax.nn.gelu(gc) * up).astype(h_ref.dtype)
    carry[...] = gate[tm - 8:tm, :]
    tail_ref[...] = gate[tm - 8:tm, :]


def _ffn_in_prompt(m2, w_ffn_in, hist8, cw, cb, B, T):
    mt, d = m2.shape
    dff = w_ffn_in.shape[1] // 2
    tm = min(1024, T)
    tn = 512
    nj = dff // tn
    ni = mt // tm
    tps = T // tm
    return pl.pallas_call(
        functools.partial(_ffn_in_seq_kernel, rows_per_seq=T),
        out_shape=[_sds((mt, dff), BF16), _sds((B, 8, dff), F32)],
        grid=(nj, ni),
        in_specs=[pl.BlockSpec((tm, d), lambda j, i: (i, 0)),
                  pl.BlockSpec((d, tn), lambda j, i: (0, j)),
                  pl.BlockSpec((d, tn), lambda j, i: (0, nj + j)),
                  pl.BlockSpec((None, 8, tn), lambda j, i: (i // tps, 0, j)),
                  pl.BlockSpec((3, tn), lambda j, i: (0, j)),
                  pl.BlockSpec((1, tn), lambda j, i: (0, j))],
        out_specs=[pl.BlockSpec((tm, tn), lambda j, i: (i, j)),
                   pl.BlockSpec((None, 8, tn), lambda j, i: (i // tps, 0, j))],
        scratch_shapes=[pltpu.VMEM((8, tn), F32)],
        compiler_params=_cp(("arbitrary", "arbitrary")),
        name="ffn_in_prompt",
    )(m2, w_ffn_in, w_ffn_in, hist8, cw, cb)


def _ffn_in_row_kernel(a_ref, wg_ref, wu_ref, hist_ref, cw_ref, cb_ref, h_ref, gate_ref):
    a = a_ref[...]
    gate = _dot(a, wg_ref[...])
    up = _dot(a, wu_ref[...])
    gc = cb_ref[...] + hist_ref[0] * cw_ref[0:1, :] + hist_ref[1] * cw_ref[1:2, :] + gate * cw_ref[2:3, :]
    h_ref[...] = (jax.nn.gelu(gc) * up).astype(h_ref.dtype)
    gate_ref[...] = gate


def _ffn_in_sample(m2, w_ffn_in, hist_t, cw, cb):
    s, d = m2.shape
    dff = w_ffn_in.shape[1] // 2
    tn = 512
    nj = dff // tn
    return pl.pallas_call(
        _ffn_in_row_kernel,
        out_shape=[_sds((s, dff), BF16), _sds((s, dff), F32)],
        grid=(nj,),
        in_specs=[pl.BlockSpec((s, d), lambda j: (0, 0)),
                  pl.BlockSpec((d, tn), lambda j: (0, j)),
                  pl.BlockSpec((d, tn), lambda j: (0, nj + j)),
                  pl.BlockSpec((2, s, tn), lambda j: (0, 0, j)),
                  pl.BlockSpec((3, tn), lambda j: (0, j)),
                  pl.BlockSpec((1, tn), lambda j: (0, j))],
        out_specs=[pl.BlockSpec((s, tn), lambda j: (0, j)), pl.BlockSpec((s, tn), lambda j: (0, j))],
        compiler_params=_cp(("arbitrary",)),
        name="ffn_in_sample",
    )(m2, w_ffn_in, w_ffn_in, hist_t, cw, cb)


def _sscore_kernel(pt_ref, q_ref, w_ref, knew_ref, cache_ref, o_ref, kbuf, sem, *, n_pages, scale):
    s = pl.program_id(0)
    ns = pl.num_programs(0)
    page = kbuf.shape[2]

    def copies(seq, slot):
        return [pltpu.make_async_copy(cache_ref.at[pt_ref[seq, pg]], kbuf.at[slot, pg], sem.at[slot])
                for pg in range(n_pages)]

    @pl.when(s == 0)
    def _():
        for cp in copies(0, 0):
            cp.start()

    @pl.when(s + 1 < ns)
    def _():
        for cp in copies(s + 1, (s + 1) % 2):
            cp.start()

    slot = s % 2
    for cp in copies(s, slot):
        cp.wait()

    q = q_ref[...]
    qh, ql = _split_bf16(q)
    w = w_ref[...] * scale
    for pg in range(n_pages):
        k = kbuf[slot, pg]
        kh, kl = _split_bf16(k)
        x = _dot_nt(qh, kh) + (_dot_nt(ql, kh) + _dot_nt(qh, kl))
        o_ref[:, pg * page:(pg + 1) * page] = jnp.sum(jnp.maximum(x, 0.0) * w, axis=0, keepdims=True)
    xs = jnp.sum(q * knew_ref[...], axis=1, keepdims=True)
    s_self = jnp.sum(jnp.maximum(xs, 0.0) * w, axis=0, keepdims=True)
    lane = lax.broadcasted_iota(I32, (1, LANES), 1)
    o_ref[:, n_pages * page:] = jnp.where(lane == 0, s_self, NEG_INF)


def _sample_scores(page_table, qi3, wi3, knew3, cache_kidx, scale):
    s, n_pages = page_table.shape
    _, page, idim = cache_kidx.shape
    h = qi3.shape[1]
    width = n_pages * page + LANES
    grid_spec = pltpu.PrefetchScalarGridSpec(
        num_scalar_prefetch=1,
        grid=(s,),
        in_specs=[pl.BlockSpec((None, h, idim), lambda i, pt: (i, 0, 0)),
                  pl.BlockSpec((None, h, 1), lambda i, pt: (i, 0, 0)),
                  pl.BlockSpec((None, 1, idim), lambda i, pt: (i, 0, 0)),
                  pl.BlockSpec(memory_space=pl.ANY)],
        out_specs=pl.BlockSpec((None, 1, width), lambda i, pt: (i, 0, 0)),
        scratch_shapes=[pltpu.VMEM((2, n_pages, page, idim), F32), pltpu.SemaphoreType.DMA((2,))],
    )
    return pl.pallas_call(
        functools.partial(_sscore_kernel, n_pages=n_pages, scale=scale),
        out_shape=_sds((s, 1, width), F32),
        grid_spec=grid_spec,
        compiler_params=_cp(("arbitrary",)),
        name="sample_scores",
    )(page_table, qi3, wi3, knew3, cache_kidx)


def _sselect_kernel(sc_ref, idx_ref, rank_s, *, kk, n_valid):
    s, width = sc_ref.shape
    nblk = width // LANES
    pos = lax.broadcasted_iota(I32, (s, width), 1)
    keys = jnp.where(pos < n_valid, _sortable(sc_ref[...] + 0.0), INT_MIN)
    cnt = lambda pred: jnp.sum(pred.astype(I32), axis=1, keepdims=True)
    thr = _kth_largest(lambda trial: cnt(keys >= trial), kk, s)
    gt = keys > thr
    eq = keys == thr
    need = (kk - cnt(gt)).astype(F32)
    tri = (lax.broadcasted_iota(I32, (LANES, LANES), 0) < lax.broadcasted_iota(I32, (LANES, LANES), 1)).astype(BF16)
    seen_eq = jnp.zeros((s, 1), F32)
    seen_sel = jnp.zeros((s, 1), F32)
    for b in range(nblk):
        sl = slice(b * LANES, (b + 1) * LANES)
        eqf = jnp.where(eq[:, sl], 1.0, 0.0)
        before_eq = seen_eq + _dot(eqf.astype(BF16), tri)
        sel = gt[:, sl] | (eq[:, sl] & (before_eq < need))
        self_f = jnp.where(sel, 1.0, 0.0)
        rank = seen_sel + _dot(self_f.astype(BF16), tri)
        rank_s[:, sl] = jnp.where(sel, rank, -1.0)
        seen_eq = seen_eq + jnp.sum(eqf, axis=1, keepdims=True)
        seen_sel = seen_sel + jnp.sum(self_f, axis=1, keepdims=True)
    jrow = lax.broadcasted_iota(I32, (width, LANES), 0)
    lcol = lax.broadcasted_iota(I32, (width, LANES), 1)
    parts = jnp.where(lcol == 0, jrow >> 7, jnp.where(lcol == 1, jrow & (LANES - 1), 0)).astype(F32).astype(BF16)
    r_iota = lax.broadcasted_iota(I32, (kk, width), 0).astype(F32)

    def body(q, carry):
        onehot = jnp.where(rank_s[pl.ds(q, 1), :] == r_iota, 1.0, 0.0).astype(BF16)
        res = _dot(onehot, parts)
        idx_ref[q] = (res[:, 0:1] * LANES + res[:, 1:2]).astype(I32)
        return carry
    lax.fori_loop(0, s, body, 0)


def _sample_select(scores, kk, n_valid):
    s, width = scores.shape
    return pl.pallas_call(
        functools.partial(_sselect_kernel, kk=kk, n_valid=n_valid),
        out_shape=_sds((s, kk, 1), I32),
        grid=(1,),
        in_specs=[pl.BlockSpec((s, width), lambda i: (0, 0))],
        out_specs=pl.BlockSpec((s, kk, 1), lambda i: (0, 0, 0)),
        scratch_shapes=[pltpu.VMEM((s, width), F32)],
        compiler_params=_cp(("arbitrary",)),
        name="sample_select",
    )(scores)


def _sattn_kernel(idx_ref, pt_ref, q_ref, idxc_ref, rbh_ref, rbl_ref, knew_ref, vnew_ref, ck_ref, cv_ref, o_ref,
                  kbuf, vbuf, sem, *, kk, past_len, page, n_buckets):
    s = pl.program_id(0)
    ns = pl.num_programs(0)
    n_heads, dh = q_ref.shape

    def issue(seq, slot):
        def body(r, carry):
            idx = idx_ref[seq, r]

            @pl.when(idx < past_len)
            def _():
                phys = pt_ref[seq, idx // page]
                off = idx % page
                pltpu.make_async_copy(ck_ref.at[phys, off], kbuf.at[slot, r], sem.at[0, slot]).start()
                pltpu.make_async_copy(cv_ref.at[phys, off], vbuf.at[slot, r], sem.at[1, slot]).start()

            @pl.when(idx >= past_len)
            def _():
                pltpu.make_async_copy(knew_ref.at[seq, 0], kbuf.at[slot, r], sem.at[0, slot]).start()
                pltpu.make_async_copy(vnew_ref.at[seq, 0], vbuf.at[slot, r], sem.at[1, slot]).start()
            return carry
        lax.fori_loop(0, kk, body, 0)

    @pl.when(s == 0)
    def _():
        issue(0, 0)

    @pl.when(s + 1 < ns)
    def _():
        issue(s + 1, (s + 1) % 2)

    slot = s % 2

    def wait_body(r, carry):
        pltpu.make_async_copy(ck_ref.at[0, 0], kbuf.at[slot, r], sem.at[0, slot]).wait()
        pltpu.make_async_copy(cv_ref.at[0, 0], vbuf.at[slot, r], sem.at[1, slot]).wait()
        return carry
    lax.fori_loop(0, kk, wait_body, 0)

    idxc = idxc_ref[...]
    dist = past_len - idxc
    bkt = _bucket_jnp(dist, n_buckets)
    onehot = jnp.where(bkt == lax.broadcasted_iota(I32, (kk, n_buckets), 1), 1.0, 0.0).astype(BF16)
    bias = _dot(onehot, rbh_ref[...]) + _dot(onehot, rbl_ref[...])
    valid = dist >= 0
    scale = dh ** -0.5
    for h in range(n_heads):
        kh = kbuf[slot, :, h, :]
        logit = jnp.sum(kh * q_ref[h:h + 1, :], axis=1, keepdims=True) * scale + bias[:, h:h + 1]
        logit = jnp.where(valid, logit, NEG_INF)
        mx = jnp.max(logit, axis=0, keepdims=True)
        p = jnp.exp(logit - mx)
        p = p / jnp.sum(p, axis=0, keepdims=True)
        o_ref[h:h + 1, :] = jnp.sum(p * vbuf[slot, :, h, :], axis=0, keepdims=True)


def _sample_attention(idx2, idx3, page_table, q3, rel_bias, knew4, vnew4, cache_k, cache_v, past_len):
    s, kk = idx2.shape
    n_heads, dh = q3.shape[1:]
    page = cache_k.shape[1]
    n_buckets = rel_bias.shape[0]
    rbh = rel_bias.astype(BF16)
    rbl = (rel_bias - rbh.astype(F32)).astype(BF16)
    grid_spec = pltpu.PrefetchScalarGridSpec(
        num_scalar_prefetch=2,
        grid=(s,),
        in_specs=[pl.BlockSpec((None, n_heads, dh), lambda i, a, b: (i, 0, 0)),
                  pl.BlockSpec((None, kk, 1), lambda i, a, b: (i, 0, 0)),
                  pl.BlockSpec((n_buckets, n_heads), lambda i, a, b: (0, 0)),
                  pl.BlockSpec((n_buckets, n_heads), lambda i, a, b: (0, 0)),
                  pl.BlockSpec(memory_space=pl.ANY), pl.BlockSpec(memory_space=pl.ANY),
                  pl.BlockSpec(memory_space=pl.ANY), pl.BlockSpec(memory_space=pl.ANY)],
        out_specs=pl.BlockSpec((None, n_heads, dh), lambda i, a, b: (i, 0, 0)),
        scratch_shapes=[pltpu.VMEM((2, kk, n_heads, dh), F32), pltpu.VMEM((2, kk, n_heads, dh), F32),
                        pltpu.SemaphoreType.DMA((2, 2))],
    )
    return pl.pallas_call(
        functools.partial(_sattn_kernel, kk=kk, past_len=past_len, page=page, n_buckets=n_buckets),
        out_shape=_sds((s, n_heads, dh), F32),
        grid_spec=grid_spec,
        compiler_params=_cp(("arbitrary",)),
        name="sample_attention",
    )(idx2, page_table, q3, idx3, rbh, rbl, knew4, vnew4, cache_k, cache_v)


def kernel(x_prompt, x_sample, cache_k, cache_v, cache_kidx, page_table, state_lru_h, state_lru_conv, state_ffn_conv,
           c_prompt, c_sample, w_ada, b_ada, w_in, lru_conv_w, lru_conv_b, lru_w_a, lru_b_a, lru_w_x, lru_b_x,
           lru_lambda, attn_rel_bias, lru_out_g, attn_out_g, w_out, ln1_g, ln1_b, w_ffn_in, ffn_conv_w, ffn_conv_b,
           w_ffn_out, ln2_g, ln2_b):
    B, T, D = x_prompt.shape
    S, ts, _ = x_sample.shape
    assert ts == 1
    _, page, n_heads, dh = cache_k.shape
    idim = cache_kidx.shape[-1]
    n_pages = page_table.shape[1]
    past_len = n_pages * page
    W = lru_conv_b.shape[0]
    aw = n_heads * dh
    n_idx_heads = (w_in.shape[1] - 2 * W - 3 * aw - idim) // (idim + 1)
    nqi = n_idx_heads * idim
    dff = ffn_conv_b.shape[0]
    assert W % LANES == 0 and dh == LANES and 2 * idim == LANES and S % 8 == 0

    w_in_b = w_in[:, :2 * W + 3 * aw].astype(BF16)
    c0 = 2 * W + 3 * aw
    w_idx = jnp.concatenate([w_in[:, c0:c0 + nqi], w_in[:, c0 + nqi + n_idx_heads:],
                             w_in[:, c0 + nqi:c0 + nqi + n_idx_heads],
                             jnp.zeros((D, LANES - idim - n_idx_heads), F32)], axis=1)
    w_idx_hi = w_idx.astype(BF16)
    w_idx_lo = (w_idx - w_idx_hi.astype(F32)).astype(BF16)
    w_out_b = w_out.astype(BF16)
    w_ffn_in_b = w_ffn_in.astype(BF16)
    w_ffn_out_b = w_ffn_out.astype(BF16)
    lru_p = dict(cw=lru_conv_w, cb=lru_conv_b.reshape(1, W),
                 wax=jnp.concatenate([lru_w_a, lru_w_x], axis=2).astype(BF16),
                 ba=lru_b_a.reshape(1, W), bx=lru_b_x.reshape(1, W), lam=lru_lambda.reshape(1, W),
                 g=lru_out_g.reshape(1, W))
    ln1 = (ln1_g.reshape(1, D), ln1_b.reshape(1, D))
    ln2 = (ln2_g.reshape(1, D), ln2_b.reshape(1, D))
    fcw, fcb = ffn_conv_w, ffn_conv_b.reshape(1, dff)
    attn_g = attn_out_g.reshape(1, aw)

    mp = -(-(S + B) // 8) * 8
    c_all = jnp.concatenate([c_sample, c_prompt, jnp.zeros((mp - S - B, D), F32)], axis=0)
    mod = _ada(c_all, w_ada, b_ada)

    st_p = _Stream(B, T, min(1024, T), False, S)
    st_s = _Stream(1, S, S, True, 0)
    outs = {}
    for name, st, x3 in (("p", st_p, x_prompt), ("s", st_s, x_sample.reshape(1, S, D))):
        M = st.M
        m1 = _modulate(st, x3, mod, 0, 1).reshape(M, D)
        xg, = _matmul(m1, w_in_b, 0, 2 * W, [F32], st.tm)
        q_f, q_b = _matmul(m1, w_in_b, 2 * W, aw, [F32, BF16], st.tm)
        k_f, k_b = _matmul(m1, w_in_b, 2 * W + aw, aw, [F32, BF16], st.tm)
        v_f, v_b = _matmul(m1, w_in_b, 2 * W + 2 * aw, aw, [F32, BF16], st.tm)
        qi, wk = _idx_project(st, x3, mod, w_idx_hi, w_idx_lo, nqi)
        qi = qi.reshape(M, nqi)
        wk = wk.reshape(M, LANES)
        kidx = wk[:, :idim]
        if name == "p":
            kk = min(TOPK_MAX, T // 4)
            y_lru, h_last = _lru_prompt(xg, B, T, jnp.zeros((B, 8, W), F32), jnp.zeros((B, 1, W), F32), lru_p)
            conv_state = xg.reshape(B, T, 2 * W)[:, T - 3:, :W]
            mask = _prompt_index(qi, wk, B, T, n_idx_heads, idim, kk)
            y_att = _prompt_attention(q_b, k_b, v_b, mask, attn_rel_bias, B, T, n_heads, dh)
        else:
            kk = min(TOPK_MAX, (past_len + 1) // 4)
            y_lru, h_last = _lru_sample(xg, jnp.swapaxes(state_lru_conv, 0, 1), state_lru_h, lru_p)
            conv_state = jnp.concatenate([state_lru_conv[:, 1:], xg[:, None, :W]], axis=1)
            scale = (n_idx_heads ** -0.5) * (idim ** -0.5)
            scores = _sample_scores(page_table, qi.reshape(S, n_idx_heads, idim),
                                    wk[:, idim:idim + n_idx_heads].reshape(S, n_idx_heads, 1),
                                    kidx.reshape(S, 1, idim), cache_kidx, scale)
            idx3 = _sample_select(scores.reshape(S, past_len + LANES), kk, past_len + 1)
            y_att = _sample_attention(idx3.reshape(S, kk), idx3, page_table, q_f.reshape(S, n_heads, dh),
                                      attn_rel_bias, k_f.reshape(S, 1, n_heads, dh), v_f.reshape(S, 1, n_heads, dh),
                                      cache_k, cache_v, past_len).reshape(S, aw)
        y_att_n = _rms_norm(y_att, attn_g, 512)
        mix = _matmul_cat(y_lru, y_att_n, w_out_b, st.tm)
        x1, m2 = _residual_ln(st, x3, mix.reshape(x3.shape), mod, 2, *ln1, mod_next=(3, 4))
        m2 = m2.reshape(M, D)
        if name == "p":
            hmid, tail = _ffn_in_prompt(m2, w_ffn_in_b, jnp.zeros((B, 8, dff), F32), fcw, fcb, B, T)
            ffn_state = tail[:, 6:, :]
        else:
            hmid, gate = _ffn_in_sample(m2, w_ffn_in_b, jnp.swapaxes(state_ffn_conv, 0, 1), fcw, fcb)
            ffn_state = jnp.concatenate([state_ffn_conv[:, 1:], gate[:, None, :]], axis=1)
        f = _matmul_ktiled(hmid, w_ffn_out_b, st.tm, 2048, 2048)
        y = _residual_ln(st, x1, f.reshape(x3.shape), mod, 5, *ln2)
        outs[name] = (y, k_f, v_f, kidx, h_last, conv_state, ffn_state)

    yp, kp, vp, kip, hp_, cp_, fp = outs["p"]
    ys, ks, vs, kis, hs_, cs_, fs = outs["s"]
    return (yp, ys.reshape(S, 1, D),
            kp.reshape(B, T, n_heads, dh), vp.reshape(B, T, n_heads, dh), kip.reshape(B, T, idim), hp_, cp_, fp,
            ks.reshape(S, 1, n_heads, dh), vs.reshape(S, 1, n_heads, dh), kis.reshape(S, 1, idim), hs_, cs_, fs)
```

```python
import functools
import math

import numpy as np
import jax
import jax.numpy as jnp
from jax import lax
from jax.experimental import pallas as pl
from jax.experimental.pallas import tpu as pltpu

F32 = jnp.float32
BF16 = jnp.bfloat16
I32 = jnp.int32

LRU_C = 8.0
TOPK_MAX = 256
MAX_DISTANCE = 128
LN_EPS = 1e-5
DEPTH = 1
DEEPNORM_ALPHA = (2.0 * DEPTH) ** 0.25
LANES = 128
VMEM_LIMIT = 56 * 1024 * 1024
INT_MIN = -(2 ** 31)
NEG_INF = float("-inf")


def _cp(sem, vmem=VMEM_LIMIT):
    return pltpu.CompilerParams(dimension_semantics=sem, vmem_limit_bytes=vmem)


def _sds(shape, dtype):
    return jax.ShapeDtypeStruct(shape, dtype)


def _split_bf16(x):
    hi = x.astype(BF16)
    lo = (x - hi.astype(F32)).astype(BF16)
    return hi, lo


def _dot(a, b):
    return jnp.dot(a, b, preferred_element_type=F32)


def _dot_nt(a, b):
    return lax.dot_general(a, b, (((1,), (1,)), ((), ())), preferred_element_type=F32)


def _dot3(a, b):
    ah, al = _split_bf16(a)
    bh, bl = _split_bf16(b)
    return _dot(ah, bh) + (_dot(al, bh) + _dot(ah, bl))


def _sortable(x):
    b = pltpu.bitcast(x, I32)
    return b ^ ((b >> 31) & 0x7FFFFFFF)


class _Stream:
    def __init__(self, G, R, tm, per_row, mod_row0):
        self.G, self.R, self.tm, self.per_row, self.mod_row0 = G, R, tm, per_row, mod_row0
        self.M = G * R
        self.nr = R // tm

    def mod_operand(self, mod2):
        if self.per_row:
            return mod2
        mp, n6 = mod2.shape
        return mod2.reshape(mp, 6, 1, n6 // 6)

    def mod_spec(self, which, width, d_model, col=lambda *ids: 0):
        if self.per_row:
            nb = d_model // width
            return pl.BlockSpec((self.tm, width), lambda g, i, *r: (i, which * nb + col(g, i, *r)))
        r0 = self.mod_row0
        return pl.BlockSpec((None, None, 1, width), lambda g, i, *r: (r0 + g, which, 0, col(g, i, *r)))


def _ada_kernel(c_ref, w_ref, b_ref, o_ref):
    c = c_ref[...]
    a = c * jax.nn.sigmoid(c)
    o_ref[...] = _dot3(a, w_ref[...]) + b_ref[...]


def _ada(c_all, w_ada, b_ada):
    mp, d = c_all.shape
    n = w_ada.shape[1]
    tn = 512
    return pl.pallas_call(
        _ada_kernel,
        out_shape=_sds((mp, n), F32),
        grid=(n // tn,),
        in_specs=[pl.BlockSpec((mp, d), lambda j: (0, 0)),
                  pl.BlockSpec((d, tn), lambda j: (0, j)),
                  pl.BlockSpec((1, tn), lambda j: (0, j))],
        out_specs=pl.BlockSpec((mp, tn), lambda j: (0, j)),
        compiler_params=_cp(("arbitrary",)),
        name="ada_mod",
    )(c_all, w_ada, b_ada.reshape(1, n))


def _mod_kernel(x_ref, sh_ref, sc_ref, o_ref):
    o_ref[...] = (x_ref[...] * (1.0 + sc_ref[...]) + sh_ref[...]).astype(o_ref.dtype)


def _modulate(st, x3, mod2, sh_which, sc_which):
    d = x3.shape[-1]
    tm = min(st.tm, 512)
    st2 = _Stream(st.G, st.R, tm, st.per_row, st.mod_row0)
    modop = st2.mod_operand(mod2)
    return pl.pallas_call(
        _mod_kernel,
        out_shape=_sds((st.G, st.R, d), BF16),
        grid=(st.G, st.R // tm),
        in_specs=[pl.BlockSpec((None, tm, d), lambda g, i: (g, i, 0)),
                  st2.mod_spec(sh_which, d, d), st2.mod_spec(sc_which, d, d)],
        out_specs=pl.BlockSpec((None, tm, d), lambda g, i: (g, i, 0)),
        compiler_params=_cp(("arbitrary", "arbitrary")),
        name="modulate",
    )(x3, modop, modop)


def _mm_nt_kernel(a_ref, bt_ref, *o_refs):
    r = _dot_nt(a_ref[...], bt_ref[...])
    for o in o_refs:
        o[...] = r.astype(o.dtype)


def _matmul_nt(a, bt, col0, n, out_dtypes, tm, tn=1024):
    m, k = a.shape
    tm = min(tm, m)
    tn = min(tn, n)
    assert m % tm == 0 and n % tn == 0 and col0 % tn == 0
    cb = col0 // tn
    outs = pl.pallas_call(
        _mm_nt_kernel,
        out_shape=[_sds((m, n), dt) for dt in out_dtypes],
        grid=(m // tm, n // tn),
        in_specs=[pl.BlockSpec((tm, k), lambda i, j: (i, 0)),
                  pl.BlockSpec((tn, k), lambda i, j: (cb + j, 0))],
        out_specs=[pl.BlockSpec((tm, tn), lambda i, j: (i, j)) for _ in out_dtypes],
        compiler_params=_cp(("arbitrary", "arbitrary")),
        name="matmul_nt",
    )(a, bt)
    return outs


def _mm2_kernel(a1_ref, a2_ref, b_ref, o_ref):
    k1 = a1_ref.shape[1]
    o_ref[...] = _dot(a1_ref[...], b_ref[0:k1, :]) + _dot(a2_ref[...], b_ref[k1:, :])


def _matmul_cat(a1, a2, b, tm, tn=1024):
    m, k1 = a1.shape
    k2 = a2.shape[1]
    n = b.shape[1]
    tm = min(tm, m)
    return pl.pallas_call(
        _mm2_kernel,
        out_shape=_sds((m, n), F32),
        grid=(m // tm, n // tn),
        in_specs=[pl.BlockSpec((tm, k1), lambda i, j: (i, 0)),
                  pl.BlockSpec((tm, k2), lambda i, j: (i, 0)),
                  pl.BlockSpec((k1 + k2, tn), lambda i, j: (0, j))],
        out_specs=pl.BlockSpec((tm, tn), lambda i, j: (i, j)),
        compiler_params=_cp(("arbitrary", "arbitrary")),
        name="matmul_cat",
    )(a1, a2, b)


def _mmk_kernel(a_ref, b_ref, o_ref):
    part = _dot(a_ref[...], b_ref[...])

    @pl.when(pl.program_id(2) == 0)
    def _():
        o_ref[...] = part

    @pl.when(pl.program_id(2) > 0)
    def _():
        o_ref[...] += part


def _matmul_ktiled(a, b, tm, tn, tk):
    m, k = a.shape
    n = b.shape[1]
    tm = min(tm, m)
    return pl.pallas_call(
        _mmk_kernel,
        out_shape=_sds((m, n), F32),
        grid=(m // tm, n // tn, k // tk),
        in_specs=[pl.BlockSpec((tm, tk), lambda i, j, kk: (i, kk)),
                  pl.BlockSpec((tk, tn), lambda i, j, kk: (kk, j))],
        out_specs=pl.BlockSpec((tm, tn), lambda i, j, kk: (i, j)),
        compiler_params=_cp(("arbitrary", "arbitrary", "arbitrary")),
        name="matmul_ktiled",
    )(a, b)


def _idxproj_kernel(x_ref, sh_ref, sc_ref, wh_ref, wl_ref, q_ref, wk_ref, acc_ref, *, nq):
    kk = pl.program_id(2)
    m = x_ref[...] * (1.0 + sc_ref[...]) + sh_ref[...]
    mh, ml = _split_bf16(m)
    wh = wh_ref[...]
    part = _dot_nt(mh, wh) + (_dot_nt(ml, wh) + _dot_nt(mh, wl_ref[...]))

    @pl.when(kk == 0)
    def _():
        acc_ref[...] = part

    @pl.when(kk > 0)
    def _():
        acc_ref[...] += part

    @pl.when(kk == pl.num_programs(2) - 1)
    def _():
        q_ref[...] = acc_ref[:, 0:nq]
        wk_ref[...] = acc_ref[:, nq:]


def _idx_project(st, x3, mod2, w_hi, w_lo, nq):
    d = x3.shape[-1]
    nw = w_hi.shape[0]
    tm = min(st.tm, 512)
    tk = 1024
    st2 = _Stream(st.G, st.R, tm, st.per_row, st.mod_row0)
    modop = st2.mod_operand(mod2)
    kcol = lambda g, i, kk: kk
    q, wk = pl.pallas_call(
        functools.partial(_idxproj_kernel, nq=nq),
        out_shape=[_sds((st.G, st.R, nq), F32), _sds((st.G, st.R, nw - nq), F32)],
        grid=(st.G, st.R // tm, d // tk),
        in_specs=[pl.BlockSpec((None, tm, tk), lambda g, i, kk: (g, i, kk)),
                  st2.mod_spec(0, tk, d, kcol), st2.mod_spec(1, tk, d, kcol),
                  pl.BlockSpec((nw, tk), lambda g, i, kk: (0, kk)),
                  pl.BlockSpec((nw, tk), lambda g, i, kk: (0, kk))],
        out_specs=[pl.BlockSpec((None, tm, nq), lambda g, i, kk: (g, i, 0)),
                   pl.BlockSpec((None, tm, nw - nq), lambda g, i, kk: (g, i, 0))],
        scratch_shapes=[pltpu.VMEM((tm, nw), F32)],
        compiler_params=_cp(("arbitrary", "arbitrary", "arbitrary")),
        name="idx_project",
    )(x3, modop, modop, w_hi, w_lo)
    return q, wk


def _softplus(z):
    return jnp.maximum(z, 0.0) + jnp.log1p(jnp.exp(-jnp.abs(z)))


def _expm1(x):
    poly = x * (1.0 + x * (1.0 / 2) * (1.0 + x * (1.0 / 3) * (1.0 + x * (1.0 / 4) * (1.0 + x * (1.0 / 5) * (1.0 + x * (1.0 / 6))))))
    return jnp.where(jnp.abs(x) < 0.25, poly, jnp.exp(x) - 1.0)


def _lru_gates(xc, wax, ba, bx, lam):
    ri = _dot(xc.astype(BF16), wax)
    r = jax.nn.sigmoid(ri[:, :LANES] + ba)
    ig = jax.nn.sigmoid(ri[:, LANES:] + bx)
    log_a = (-LRU_C * r) * _softplus(-lam)
    a = jnp.exp(log_a)
    u = jnp.sqrt(-_expm1(2.0 * log_a)) * (ig * xc)
    return a, u


def _lru_prompt_kernel(xr_ref, gr_ref, hist_ref, h0_ref, cw_ref, cb_ref, wax_ref, ba_ref, bx_ref, lam_ref, g_ref,
                       y_ref, hlast_ref, xbuf, hcar, ybuf):
    t = pl.program_id(1)
    tc, w = xr_ref.shape
    nb = w // LANES

    @pl.when(t == 0)
    def _():
        xbuf[0:8, :] = hist_ref[...]
        hcar[...] = h0_ref[...]

    @pl.when(t > 0)
    def _():
        xbuf[0:8, :] = xbuf[tc:tc + 8, :]

    xbuf[8:8 + tc, :] = xr_ref[...]
    rows = lax.broadcasted_iota(I32, (tc, LANES), 0)
    ssq = jnp.zeros((tc, 1), F32)
    for n in range(nb):
        sl = slice(n * LANES, (n + 1) * LANES)
        xc = cb_ref[:, sl]
        for j in range(4):
            xc = xc + xbuf[5 + j:5 + j + tc, sl] * cw_ref[j:j + 1, sl]
        a, u = _lru_gates(xc, wax_ref[n], ba_ref[:, sl], bx_ref[:, sl], lam_ref[:, sl])
        s = 1
        while s < tc:
            keep = rows >= s
            u = jnp.where(keep, a * pltpu.roll(u, s, axis=0) + u, u)
            a = jnp.where(keep, a * pltpu.roll(a, s, axis=0), a)
            s *= 2
        h = a * hcar[:, sl] + u
        hcar[:, sl] = h[tc - 1:tc, :]
        y = h * jax.nn.gelu(gr_ref[:, sl])
        ybuf[:, sl] = y
        ssq = ssq + jnp.sum(y * y, axis=1, keepdims=True)
    scale = lax.rsqrt(ssq * (1.0 / w) + LN_EPS)
    y_ref[...] = (ybuf[...] * scale * g_ref[...]).astype(y_ref.dtype)
    hlast_ref[...] = hcar[...]


def _lru_prompt(xg, B, T, hist8, h0, p):
    w = xg.shape[1] // 2
    tc = min(256, T)
    nt = T // tc
    vec = lambda: pl.BlockSpec((1, w), lambda b, t: (0, 0))
    y, hl = pl.pallas_call(
        _lru_prompt_kernel,
        out_shape=[_sds((B * T, w), BF16), _sds((B, 1, w), F32)],
        grid=(B, nt),
        in_specs=[pl.BlockSpec((tc, w), lambda b, t: (b * nt + t, 0)),
                  pl.BlockSpec((tc, w), lambda b, t: (b * nt + t, 1)),
                  pl.BlockSpec((None, 8, w), lambda b, t: (b, 0, 0)),
                  pl.BlockSpec((None, 1, w), lambda b, t: (b, 0, 0)),
                  pl.BlockSpec((4, w), lambda b, t: (0, 0)), vec(),
                  pl.BlockSpec(p["wax"].shape, lambda b, t: (0, 0, 0)),
                  vec(), vec(), vec(), vec()],
        out_specs=[pl.BlockSpec((tc, w), lambda b, t: (b * nt + t, 0)),
                   pl.BlockSpec((None, 1, w), lambda b, t: (b, 0, 0))],
        scratch_shapes=[pltpu.VMEM((tc + 8, w), F32), pltpu.VMEM((1, w), F32), pltpu.VMEM((tc, w), F32)],
        compiler_params=_cp(("arbitrary", "arbitrary")),
        name="rglru_prompt",
    )(xg, xg, hist8, h0, p["cw"], p["cb"], p["wax"], p["ba"], p["bx"], p["lam"], p["g"])
    return y, hl.reshape(B, w)


def _lru_sample_kernel(xr_ref, gr_ref, hist_ref, h0_ref, cw_ref, cb_ref, wax_ref, ba_ref, bx_ref, lam_ref, g_ref,
                       y_ref, h_ref, ybuf):
    r, w = xr_ref.shape
    nb = w // LANES
    ssq = jnp.zeros((r, 1), F32)
    for n in range(nb):
        sl = slice(n * LANES, (n + 1) * LANES)
        xc = cb_ref[:, sl]
        for j in range(3):
            xc = xc + hist_ref[j, :, sl] * cw_ref[j:j + 1, sl]
        xc = xc + xr_ref[:, sl] * cw_ref[3:4, sl]
        a, u = _lru_gates(xc, wax_ref[n], ba_ref[:, sl], bx_ref[:, sl], lam_ref[:, sl])
        h = a * h0_ref[:, sl] + u
        h_ref[:, sl] = h
        y = h * jax.nn.gelu(gr_ref[:, sl])
        ybuf[:, sl] = y
        ssq = ssq + jnp.sum(y * y, axis=1, keepdims=True)
    scale = lax.rsqrt(ssq * (1.0 / w) + LN_EPS)
    y_ref[...] = (ybuf[...] * scale * g_ref[...]).astype(y_ref.dtype)


def _lru_sample(xg, hist_t, h0, p):
    s = xg.shape[0]
    w = xg.shape[1] // 2
    vec = lambda: pl.BlockSpec((1, w), lambda i: (0, 0))
    return pl.pallas_call(
        _lru_sample_kernel,
        out_shape=[_sds((s, w), BF16), _sds((s, w), F32)],
        grid=(1,),
        in_specs=[pl.BlockSpec((s, w), lambda i: (0, 0)), pl.BlockSpec((s, w), lambda i: (0, 1)),
                  pl.BlockSpec((3, s, w), lambda i: (0, 0, 0)), pl.BlockSpec((s, w), lambda i: (0, 0)),
                  pl.BlockSpec((4, w), lambda i: (0, 0)), vec(),
                  pl.BlockSpec(p["wax"].shape, lambda i: (0, 0, 0)),
                  vec(), vec(), vec(), vec()],
        out_specs=[pl.BlockSpec((s, w), lambda i: (0, 0)), pl.BlockSpec((s, w), lambda i: (0, 0))],
        scratch_shapes=[pltpu.VMEM((s, w), F32)],
        compiler_params=_cp(("arbitrary",)),
        name="rglru_sample",
    )(xg, xg, hist_t, h0, p["cw"], p["cb"], p["wax"], p["ba"], p["bx"], p["lam"], p["g"])


def _kth_largest(count_ge, kk, rows):
    def body(p, thr):
        bit = jnp.left_shift(jnp.int32(1), 31 - p)
        trial = thr + bit
        return jnp.where(count_ge(trial) >= kk, trial, thr)
    return lax.fori_loop(0, 32, body, jnp.full((rows, 1), INT_MIN, I32))


def _pidx_kernel(q_ref, wkq_ref, wkall_ref, o_ref, kcat, qcat, wb, keys, *, n_heads, idim, kk, kc):
    i = pl.program_id(1)
    tq = q_ref.shape[0]
    t_all = wkall_ref.shape[0]
    lane = lax.broadcasted_iota(I32, (tq, LANES), 1)

    @pl.when(i == 0)
    def _():
        k = wkall_ref[...]
        kh = k.astype(BF16).astype(F32)
        kl = k - kh
        left = lax.broadcasted_iota(I32, k.shape, 1) < idim
        kcat[:, 0:LANES] = jnp.where(left, kh, pltpu.roll(kl, idim, axis=1)).astype(BF16)
        kcat[:, LANES:] = jnp.where(left, kh, 0.0).astype(BF16)

    for pr in range(n_heads // 2):
        v = q_ref[:, pr * LANES:(pr + 1) * LANES]
        vh = v.astype(BF16).astype(F32)
        vl = v - vh
        vh_r = pltpu.roll(vh, idim, axis=1)
        vl_r = pltpu.roll(vl, idim, axis=1)
        first = lane < idim
        qcat[2 * pr, :, 0:LANES] = jnp.where(first, vh, vh_r).astype(BF16)
        qcat[2 * pr, :, LANES:] = jnp.where(first, vl, 0.0).astype(BF16)
        qcat[2 * pr + 1, :, 0:LANES] = jnp.where(first, vh_r, vh).astype(BF16)
        qcat[2 * pr + 1, :, LANES:] = jnp.where(first, vl_r, 0.0).astype(BF16)
    scale = (n_heads ** -0.5) * (idim ** -0.5)
    wq = wkq_ref[...] * scale
    for h in range(n_heads):
        wb[h] = jnp.broadcast_to(wq[:, idim + h:idim + h + 1], (tq, LANES))

    nch = (i * tq + tq + kc - 1) // kc
    qpos = i * tq + lax.broadcasted_iota(I32, (tq, kc), 0)
    cols = lax.broadcasted_iota(I32, (tq, kc), 1)

    def score_chunk(c, carry):
        k_c = kcat[pl.ds(pl.multiple_of(c * kc, kc), kc), :]
        x_all = _dot_nt(qcat[...].reshape(n_heads * tq, 2 * LANES), k_c)
        acc = jnp.zeros((tq, kc), F32)
        for h in range(n_heads):
            acc = acc + jnp.maximum(x_all[h * tq:(h + 1) * tq], 0.0) * jnp.tile(wb[h], (1, kc // LANES))
        key = jnp.where(c * kc + cols <= qpos, _sortable(acc + 0.0), INT_MIN)
        keys[:, pl.ds(pl.multiple_of(c * kc, kc), kc)] = key
        return carry

    lax.fori_loop(0, nch, score_chunk, 0)

    def count_where(preds):
        def body(c, cnts):
            kv = keys[:, pl.ds(pl.multiple_of(c * kc, kc), kc)]
            return [cnt + _fold_lanes(pred(kv).astype(I32), jnp.add) for cnt, pred in zip(cnts, preds)]
        cnts = lax.fori_loop(0, nch, body, [jnp.zeros((tq, LANES), I32) for _ in preds])
        return [jnp.sum(cnt, axis=1, keepdims=True) for cnt in cnts]

    thr = _kth_largest(lambda t: count_where([lambda kv: kv >= t])[0], kk, tq)
    n_ge, = count_where([lambda kv: (kv >= thr) & (kv > INT_MIN)])
    tie = jnp.max(n_ge) > kk

    o_ref[...] = jnp.full(o_ref.shape, NEG_INF, F32)

    @pl.when(jnp.logical_not(tie))
    def _():
        def body(c, carry):
            sl = pl.ds(pl.multiple_of(c * kc, kc), kc)
            kv = keys[:, sl]
            o_ref[:, sl] = jnp.where((kv >= thr) & (kv > INT_MIN), 0.0, NEG_INF)
            return carry
        lax.fori_loop(0, nch, body, 0)

    @pl.when(tie)
    def _():
        n_gt, = count_where([lambda kv: kv > thr])
        need = (kk - n_gt).astype(F32)
        tri = (lax.broadcasted_iota(I32, (kc, kc), 0) < lax.broadcasted_iota(I32, (kc, kc), 1)).astype(BF16)

        def body(c, seen):
            sl = pl.ds(pl.multiple_of(c * kc, kc), kc)
            kv = keys[:, sl]
            eq = (kv == thr) & (kv > INT_MIN)
            eqf = jnp.where(eq, 1.0, 0.0)
            before = seen + _dot(eqf.astype(BF16), tri)
            sel = (kv > thr) | (eq & (before < need))
            o_ref[:, sl] = jnp.where(sel, 0.0, NEG_INF)
            return seen + jnp.sum(eqf, axis=1, keepdims=True)
        lax.fori_loop(0, nch, body, jnp.zeros((tq, 1), F32))


def _prompt_index(qi, wk, B, T, n_heads, idim, kk):
    tq = min(256, T)
    kc = min(512, T)
    nq = T // tq
    return pl.pallas_call(
        functools.partial(_pidx_kernel, n_heads=n_heads, idim=idim, kk=kk, kc=kc),
        out_shape=_sds((B * T, T), F32),
        grid=(B, nq),
        in_specs=[pl.BlockSpec((tq, n_heads * idim), lambda b, i: (b * nq + i, 0)),
                  pl.BlockSpec((tq, LANES), lambda b, i: (b * nq + i, 0)),
                  pl.BlockSpec((T, LANES), lambda b, i: (b, 0))],
        out_specs=pl.BlockSpec((tq, T), lambda b, i: (b * nq + i, 0)),
        scratch_shapes=[pltpu.VMEM((T, 2 * LANES), BF16), pltpu.VMEM((n_heads, tq, 2 * LANES), BF16),
                        pltpu.VMEM((n_heads, tq, LANES), F32), pltpu.VMEM((tq, T), I32)],
        compiler_params=_cp(("arbitrary", "arbitrary")),
        name="prompt_index",
    )(qi, wk, wk)


def _bucket_np(d):
    n_buckets = 32
    max_exact = n_buckets // 2
    d = np.maximum(d, 0)
    large = max_exact + (np.log(np.maximum(d, 1).astype(np.float32) / np.float32(max_exact))
                         / np.float32(math.log(MAX_DISTANCE / max_exact))
                         * np.float32(n_buckets - max_exact)).astype(np.int32)
    large = np.minimum(large, n_buckets - 1)
    return np.where(d < max_exact, d, large).astype(np.int32)


def _bucket_jnp(d, n_buckets):
    max_exact = n_buckets // 2
    d = jnp.maximum(d, 0)
    large = max_exact + (jnp.log(jnp.maximum(d, 1).astype(F32) / max_exact)
                         / math.log(MAX_DISTANCE / max_exact) * (n_buckets - max_exact)).astype(I32)
    large = jnp.minimum(large, n_buckets - 1)
    return jnp.where(d < max_exact, d, large)


def _fold_lanes(x, op):
    out = x[:, 0:LANES]
    for s in range(1, x.shape[1] // LANES):
        out = op(out, x[:, s * LANES:(s + 1) * LANES])
    return out


def _pattn_kernel(rb_ref, q_ref, k_ref, v_ref, mask_ref, bkt_ref, wsrc_ref, o_ref, wdst_ref, tbl, sbuf, mx_s, l_s, acc_s,
                  *, hp, kc, n_far):
    wdst_ref[...] = wsrc_ref[...].astype(wdst_ref.dtype)
    hg = pl.program_id(1)
    i = pl.program_id(2)
    tq = q_ref.shape[0]
    dh = q_ref.shape[1] // hp
    scale = dh ** -0.5
    c_exp = scale * math.log2(math.e)
    n_buckets = rb_ref.shape[0]

    @pl.when(i == 0)
    def _():
        bkt = bkt_ref[...]
        for h in range(hp):
            far_b = rb_ref[n_far, hg * hp + h]
            acc = jnp.full(bkt.shape, far_b, F32)
            for b in range(n_buckets):
                acc = jnp.where(bkt == b, rb_ref[b, hg * hp + h], acc)
            tbl[h] = (acc - far_b) * (1.0 / scale)

    for h in range(hp):
        mx_s[h] = jnp.full((tq, LANES), NEG_INF, F32)
        l_s[h] = jnp.zeros((tq, LANES), F32)
        acc_s[h] = jnp.zeros((tq, dh), F32)

    nch = (i * tq + tq + kc - 1) // kc
    c_near = jnp.maximum((i * tq - tq) // kc, 0)

    def logits(c, bias_of):
        sl = pl.ds(pl.multiple_of(c * kc, kc), kc)
        msk = mask_ref[:, sl]
        for h in range(hp):
            hs = slice(h * dh, (h + 1) * dh)
            s = bias_of(h, _dot_nt(q_ref[:, hs], k_ref[sl, hs]) + msk)
            sbuf[h, :, sl] = s
            mx_s[h] = jnp.maximum(mx_s[h], _fold_lanes(s, jnp.maximum))

    def far(c, carry):
        logits(c, lambda h, s: s)
        return carry

    def near(c, carry):
        off = pl.multiple_of(kc - (i * tq - c * kc), LANES)
        logits(c, lambda h, s: s + tbl[h, :, pl.ds(off, kc)])
        return carry

    lax.fori_loop(0, c_near, far, 0)
    lax.fori_loop(c_near, nch, near, 0)
    for h in range(hp):
        mx_s[h] = jnp.broadcast_to(jnp.max(mx_s[h], axis=1, keepdims=True), (tq, LANES))

    def weigh(c, carry):
        sl = pl.ds(pl.multiple_of(c * kc, kc), kc)
        for h in range(hp):
            hs = slice(h * dh, (h + 1) * dh)
            m = jnp.tile(mx_s[h], (1, kc // LANES))
            p = jnp.exp2((sbuf[h, :, sl] - m) * c_exp)
            l_s[h] = l_s[h] + _fold_lanes(p, jnp.add)
            acc_s[h] = acc_s[h] + _dot(p.astype(BF16), v_ref[sl, hs])
        return carry

    lax.fori_loop(0, nch, weigh, 0)
    for h in range(hp):
        o_ref[:, h * dh:(h + 1) * dh] = acc_s[h] / jnp.sum(l_s[h], axis=1, keepdims=True)


def _prompt_attention(q, k, v, mask, rel_bias, B, T, n_heads, dh, w_cast):
    tq = 128
    kc = min(512, T)
    hp = 4
    nq = T // tq
    ng = n_heads // hp
    rc = _rider_rows(w_cast.shape[0], B * ng * nq)
    wn = w_cast.shape[1]
    n_buckets = rel_bias.shape[0]
    r = np.arange(tq)[:, None]
    x = np.arange(2 * kc)[None, :]
    dist = r + kc - x
    bkt = np.where(dist >= 0, _bucket_np(dist), -1).astype(np.int32)
    far_d = kc + tq
    assert _bucket_np(np.array([tq + 1]))[0] == n_buckets - 1 and far_d > tq
    grid_spec = pltpu.PrefetchScalarGridSpec(
        num_scalar_prefetch=0,
        grid=(B, n_heads // hp, nq),
        in_specs=[pl.BlockSpec(memory_space=pltpu.SMEM),
                  pl.BlockSpec((tq, hp * dh), lambda b, g, i: (b * nq + i, g)),
                  pl.BlockSpec((T, hp * dh), lambda b, g, i: (b, g)),
                  pl.BlockSpec((T, hp * dh), lambda b, g, i: (b, g)),
                  pl.BlockSpec((tq, T), lambda b, g, i: (b * nq + i, 0)),
                  pl.BlockSpec((tq, 2 * kc), lambda b, g, i: (0, 0)),
                  pl.BlockSpec((rc, wn), lambda b, g, i: ((b * ng + g) * nq + i, 0))],
        out_specs=[pl.BlockSpec((tq, hp * dh), lambda b, g, i: (b * nq + i, g)),
                   pl.BlockSpec((rc, wn), lambda b, g, i: ((b * ng + g) * nq + i, 0))],
        scratch_shapes=[pltpu.VMEM((hp, tq, 2 * kc), F32), pltpu.VMEM((hp, tq, T), F32),
                        pltpu.VMEM((hp, tq, LANES), F32), pltpu.VMEM((hp, tq, LANES), F32),
                        pltpu.VMEM((hp, tq, dh), F32)],
    )
    return pl.pallas_call(
        functools.partial(_pattn_kernel, hp=hp, kc=kc, n_far=n_buckets - 1),
        out_shape=[_sds((B * T, n_heads * dh), F32), _sds(w_cast.shape, BF16)],
        grid_spec=grid_spec,
        compiler_params=_cp(("arbitrary", "arbitrary", "arbitrary")),
        name="prompt_attention",
    )(rel_bias, q, k, v, mask, jnp.asarray(bkt), w_cast)


def _rms_kernel(x_ref, g_ref, o_ref):
    x = x_ref[...]
    ms = jnp.mean(x * x, axis=-1, keepdims=True)
    o_ref[...] = (x * lax.rsqrt(ms + LN_EPS) * g_ref[...]).astype(o_ref.dtype)


def _rms_norm(x, g, tm):
    m, w = x.shape
    tm = min(tm, m)
    return pl.pallas_call(
        _rms_kernel,
        out_shape=_sds((m, w), BF16),
        grid=(m // tm,),
        in_specs=[pl.BlockSpec((tm, w), lambda i: (i, 0)), pl.BlockSpec((1, w), lambda i: (0, 0))],
        out_specs=pl.BlockSpec((tm, w), lambda i: (i, 0)),
        compiler_params=_cp(("arbitrary",)),
        name="rms_norm",
    )(x, g)


def _layer_norm(z, g, b):
    mu = jnp.mean(z, axis=-1, keepdims=True)
    zc = z - mu
    var = jnp.mean(zc * zc, axis=-1, keepdims=True)
    return zc * lax.rsqrt(var + LN_EPS) * g + b


def _ln_mod_kernel(x_ref, f_ref, gate_ref, sh_ref, sc_ref, g_ref, b_ref, x1_ref, m_ref):
    z = DEEPNORM_ALPHA * x_ref[...] + gate_ref[...] * f_ref[...]
    x1 = _layer_norm(z, g_ref[...], b_ref[...])
    x1_ref[...] = x1
    m_ref[...] = (x1 * (1.0 + sc_ref[...]) + sh_ref[...]).astype(m_ref.dtype)


def _ln_kernel(x_ref, f_ref, gate_ref, g_ref, b_ref, y_ref):
    z = DEEPNORM_ALPHA * x_ref[...] + gate_ref[...] * f_ref[...]
    y_ref[...] = _layer_norm(z, g_ref[...], b_ref[...])


def _residual_ln(st, x3, f3, mod2, gate_which, ln_g, ln_b, mod_next=None):
    d = x3.shape[-1]
    tm = min(st.tm, 256)
    st2 = _Stream(st.G, st.R, tm, st.per_row, st.mod_row0)
    modop = st2.mod_operand(mod2)
    row = pl.BlockSpec((None, tm, d), lambda g, i: (g, i, 0))
    vec = pl.BlockSpec((1, d), lambda g, i: (0, 0))
    if mod_next is None:
        return pl.pallas_call(
            _ln_kernel,
            out_shape=_sds(x3.shape, F32),
            grid=(st.G, st.R // tm),
            in_specs=[row, row, st2.mod_spec(gate_which, d, d), vec, vec],
            out_specs=row,
            compiler_params=_cp(("arbitrary", "arbitrary")),
            name="residual_ln",
        )(x3, f3, modop, ln_g, ln_b)
    return pl.pallas_call(
        _ln_mod_kernel,
        out_shape=[_sds(x3.shape, F32), _sds(x3.shape, BF16)],
        grid=(st.G, st.R // tm),
        in_specs=[row, row, st2.mod_spec(gate_which, d, d), st2.mod_spec(mod_next[0], d, d),
                  st2.mod_spec(mod_next[1], d, d), vec, vec],
        out_specs=[row, row],
        compiler_params=_cp(("arbitrary", "arbitrary")),
        name="residual_ln_mod",
    )(x3, f3, modop, modop, modop, ln_g, ln_b)


def _rider_rows(n_rows, n_steps):
    rows = n_rows // n_steps
    assert rows * n_steps == n_rows and rows % 16 == 0, (n_rows, n_steps)
    return rows


def _ffn_in_seq_kernel(a_ref, wg_ref, wu_ref, hist_ref, cw_ref, cb_ref, wsrc_ref, h_ref, tail_ref, wdst_ref, carry, *,
                       rows_per_seq):
    wdst_ref[...] = wsrc_ref[...].astype(wdst_ref.dtype)
    i = pl.program_id(1)
    tm = a_ref.shape[0]
    tiles_per_seq = rows_per_seq // tm
    tn = wg_ref.shape[1]
    cn = min(tn, 2 * LANES)

    @pl.when(i % tiles_per_seq == 0)
    def _():
        carry[...] = hist_ref[...]

    rows = lax.broadcasted_iota(I32, (tm, cn), 0)
    for c in range(tn // cn):
        cs = slice(c * cn, (c + 1) * cn)
        gate = _dot(a_ref[...], wg_ref[:, cs])
        up = _dot(a_ref[...], wu_ref[:, cs])
        prev = carry[:, cs]
        g1 = jnp.where(rows >= 1, pltpu.roll(gate, 1, axis=0), prev[7:8, :])
        g2 = jnp.where(rows >= 2, pltpu.roll(gate, 2, axis=0), jnp.where(rows == 1, prev[7:8, :], prev[6:7, :]))
        gc = cb_ref[:, cs] + g2 * cw_ref[0:1, cs] + g1 * cw_ref[1:2, cs] + gate * cw_ref[2:3, cs]
        h_ref[:, cs] = (jax.nn.gelu(gc) * up).astype(h_ref.dtype)
        carry[:, cs] = gate[tm - 8:tm, :]
        tail_ref[:, cs] = gate[tm - 8:tm, :]


def _ffn_in_prompt(m2, w_ffn_in, hist8, cw, cb, B, T, w_cast):
    mt, d = m2.shape
    dff = w_ffn_in.shape[1] // 2
    tm = min(1024, T)
    tn = 512
    nj = dff // tn
    ni = mt // tm
    tps = T // tm
    rc = _rider_rows(w_cast.shape[0], nj * ni)
    wn = w_cast.shape[1]
    return pl.pallas_call(
        functools.partial(_ffn_in_seq_kernel, rows_per_seq=T),
        out_shape=[_sds((mt, dff), BF16), _sds((B, 8, dff), F32), _sds(w_cast.shape, BF16)],
        grid=(nj, ni),
        in_specs=[pl.BlockSpec((tm, d), lambda j, i: (i, 0)),
                  pl.BlockSpec((d, tn), lambda j, i: (0, j)),
                  pl.BlockSpec((d, tn), lambda j, i: (0, nj + j)),
                  pl.BlockSpec((None, 8, tn), lambda j, i: (i // tps, 0, j)),
                  pl.BlockSpec((3, tn), lambda j, i: (0, j)),
                  pl.BlockSpec((1, tn), lambda j, i: (0, j)),
                  pl.BlockSpec((rc, wn), lambda j, i: (j * ni + i, 0))],
        out_specs=[pl.BlockSpec((tm, tn), lambda j, i: (i, j)),
                   pl.BlockSpec((None, 8, tn), lambda j, i: (i // tps, 0, j)),
                   pl.BlockSpec((rc, wn), lambda j, i: (j * ni + i, 0))],
        scratch_shapes=[pltpu.VMEM((8, tn), F32)],
        compiler_params=_cp(("arbitrary", "arbitrary")),
        name="ffn_in_prompt",
    )(m2, w_ffn_in, w_ffn_in, hist8, cw, cb, w_cast)


def _ffn_in_row_kernel(a_ref, wg_ref, wu_ref, hist_ref, cw_ref, cb_ref, h_ref, gate_ref):
    a = a_ref[...]
    gate = _dot(a, wg_ref[...])
    up = _dot(a, wu_ref[...])
    gc = cb_ref[...] + hist_ref[0] * cw_ref[0:1, :] + hist_ref[1] * cw_ref[1:2, :] + gate * cw_ref[2:3, :]
    h_ref[...] = (jax.nn.gelu(gc) * up).astype(h_ref.dtype)
    gate_ref[...] = gate


def _ffn_in_sample(m2, w_ffn_in, hist_t, cw, cb):
    s, d = m2.shape
    dff = w_ffn_in.shape[1] // 2
    tn = 512
    nj = dff // tn
    return pl.pallas_call(
        _ffn_in_row_kernel,
        out_shape=[_sds((s, dff), BF16), _sds((s, dff), F32)],
        grid=(nj,),
        in_specs=[pl.BlockSpec((s, d), lambda j: (0, 0)),
                  pl.BlockSpec((d, tn), lambda j: (0, j)),
                  pl.BlockSpec((d, tn), lambda j: (0, nj + j)),
                  pl.BlockSpec((2, s, tn), lambda j: (0, 0, j)),
                  pl.BlockSpec((3, tn), lambda j: (0, j)),
                  pl.BlockSpec((1, tn), lambda j: (0, j))],
        out_specs=[pl.BlockSpec((s, tn), lambda j: (0, j)), pl.BlockSpec((s, tn), lambda j: (0, j))],
        compiler_params=_cp(("arbitrary",)),
        name="ffn_in_sample",
    )(m2, w_ffn_in, w_ffn_in, hist_t, cw, cb)


def _sscore_kernel(pt_ref, q_ref, w_ref, knew_ref, cache_ref, o_ref, kbuf, sem, *, n_pages, scale):
    s = pl.program_id(0)
    ns = pl.num_programs(0)
    page = kbuf.shape[3]

    def copies(seq, slot):
        return [pltpu.make_async_copy(cache_ref.at[pt_ref[seq, pg]], kbuf.at[slot, pg], sem.at[slot])
                for pg in range(n_pages)]

    @pl.when(s == 0)
    def _():
        for cp in copies(0, 0):
            cp.start()

    @pl.when(s + 1 < ns)
    def _():
        for cp in copies(s + 1, (s + 1) % 2):
            cp.start()

    slot = s % 2
    for cp in copies(s, slot):
        cp.wait()

    q = q_ref[...]
    qh = q.astype(BF16).astype(F32)
    q3 = jnp.concatenate([qh, q - qh, qh], axis=1).astype(BF16)
    w = w_ref[...] * scale
    for pg in range(n_pages):
        kt = kbuf[slot, pg]
        kh, kl = _split_bf16(kt)
        x = _dot(q3, jnp.concatenate([kh, kh, kl], axis=0))
        o_ref[:, pg * page:(pg + 1) * page] = jnp.sum(jnp.maximum(x, 0.0) * w, axis=0, keepdims=True)
    xs = jnp.sum(q * knew_ref[...], axis=1, keepdims=True)
    s_self = jnp.sum(jnp.maximum(xs, 0.0) * w, axis=0, keepdims=True)
    lane = lax.broadcasted_iota(I32, (1, LANES), 1)
    o_ref[:, n_pages * page:] = jnp.where(lane == 0, s_self, NEG_INF)


def _sample_scores(page_table, qi3, wi3, knew3, cache_kidx_t, scale):
    s, n_pages = page_table.shape
    _, idim, page = cache_kidx_t.shape
    h = qi3.shape[1]
    width = n_pages * page + LANES
    grid_spec = pltpu.PrefetchScalarGridSpec(
        num_scalar_prefetch=1,
        grid=(s,),
        in_specs=[pl.BlockSpec((None, h, idim), lambda i, pt: (i, 0, 0)),
                  pl.BlockSpec((None, h, 1), lambda i, pt: (i, 0, 0)),
                  pl.BlockSpec((None, 1, idim), lambda i, pt: (i, 0, 0)),
                  pl.BlockSpec(memory_space=pl.ANY)],
        out_specs=pl.BlockSpec((None, 1, width), lambda i, pt: (i, 0, 0)),
        scratch_shapes=[pltpu.VMEM((2, n_pages, idim, page), F32), pltpu.SemaphoreType.DMA((2,))],
    )
    return pl.pallas_call(
        functools.partial(_sscore_kernel, n_pages=n_pages, scale=scale),
        out_shape=_sds((s, 1, width), F32),
        grid_spec=grid_spec,
        compiler_params=_cp(("arbitrary",)),
        name="sample_scores",
    )(page_table, qi3, wi3, knew3, cache_kidx_t)


def _sselect_kernel(sc_ref, idx_ref, rank_s, *, kk, n_valid):
    s, width = sc_ref.shape
    nblk = width // LANES
    pos = lax.broadcasted_iota(I32, (s, width), 1)
    keys = jnp.where(pos < n_valid, _sortable(sc_ref[...] + 0.0), INT_MIN)
    cnt = lambda pred: jnp.sum(pred.astype(I32), axis=1, keepdims=True)
    thr = _kth_largest(lambda t: cnt(keys >= t), kk, s)
    gt = keys > thr
    eq = keys == thr
    need = (kk - cnt(gt)).astype(F32)
    tri = (lax.broadcasted_iota(I32, (LANES, LANES), 0) < lax.broadcasted_iota(I32, (LANES, LANES), 1)).astype(BF16)
    seen_eq = jnp.zeros((s, 1), F32)
    seen_sel = jnp.zeros((s, 1), F32)
    for b in range(nblk):
        sl = slice(b * LANES, (b + 1) * LANES)
        eqf = jnp.where(eq[:, sl], 1.0, 0.0)
        before_eq = seen_eq + _dot(eqf.astype(BF16), tri)
        sel = gt[:, sl] | (eq[:, sl] & (before_eq < need))
        self_f = jnp.where(sel, 1.0, 0.0)
        rank = seen_sel + _dot(self_f.astype(BF16), tri)
        rank_s[:, sl] = jnp.where(sel, rank, -1.0)
        seen_eq = seen_eq + jnp.sum(eqf, axis=1, keepdims=True)
        seen_sel = seen_sel + jnp.sum(self_f, axis=1, keepdims=True)
    jrow = lax.broadcasted_iota(I32, (width, LANES), 0)
    lcol = lax.broadcasted_iota(I32, (width, LANES), 1)
    parts = jnp.where(lcol == 0, jrow >> 7, jnp.where(lcol == 1, jrow & (LANES - 1), 0)).astype(F32).astype(BF16)
    r_iota = lax.broadcasted_iota(I32, (kk, width), 0).astype(F32)

    def body(q, carry):
        onehot = jnp.where(rank_s[pl.ds(q, 1), :] == r_iota, 1.0, 0.0).astype(BF16)
        res = _dot(onehot, parts)
        idx_ref[q] = (res[:, 0:1] * LANES + res[:, 1:2]).astype(I32)
        return carry
    lax.fori_loop(0, s, body, 0)


def _sample_select(scores, kk, n_valid):
    s, width = scores.shape
    return pl.pallas_call(
        functools.partial(_sselect_kernel, kk=kk, n_valid=n_valid),
        out_shape=_sds((s, kk, 1), I32),
        grid=(1,),
        in_specs=[pl.BlockSpec((s, width), lambda i: (0, 0))],
        out_specs=pl.BlockSpec((s, kk, 1), lambda i: (0, 0, 0)),
        scratch_shapes=[pltpu.VMEM((s, width), F32)],
        compiler_params=_cp(("arbitrary",)),
        name="sample_select",
    )(scores)


def _sattn_kernel(idx_ref, pt_ref, q_ref, idxc_ref, idxr_ref, rbh_ref, rbl_ref, knew_ref, vnew_ref, ck_ref, cv_ref, o_ref,
                  kbuf, vbuf, sem, *, kk, past_len, page, n_buckets):
    s = pl.program_id(0)
    ns = pl.num_programs(0)
    n_heads, dh = q_ref.shape
    page_bits = page.bit_length() - 1

    def issue(seq, slot):
        def body(r, carry):
            idx = jnp.minimum(idx_ref[seq, r], past_len - 1)
            phys = pt_ref[seq, lax.shift_right_logical(idx, page_bits)]
            off = idx & (page - 1)
            pltpu.make_async_copy(ck_ref.at[phys, off], kbuf.at[slot, r], sem.at[0, slot]).start()
            pltpu.make_async_copy(cv_ref.at[phys, off], vbuf.at[slot, r], sem.at[1, slot]).start()
            return carry
        lax.fori_loop(0, kk, body, 0, unroll=8)

    @pl.when(s == 0)
    def _():
        issue(0, 0)

    @pl.when(s + 1 < ns)
    def _():
        issue(s + 1, (s + 1) % 2)

    slot = s % 2
    pltpu.make_async_copy(kbuf.at[slot], kbuf.at[slot], sem.at[0, slot]).wait()
    pltpu.make_async_copy(vbuf.at[slot], vbuf.at[slot], sem.at[1, slot]).wait()

    idxr = idxr_ref[...]
    dist = past_len - idxr
    bkt = _bucket_jnp(dist, n_buckets)
    onehot = jnp.where(bkt == lax.broadcasted_iota(I32, (n_buckets, kk), 0), 1.0, 0.0).astype(BF16)
    bias = _dot(rbh_ref[...], onehot) + _dot(rbl_ref[...], onehot)
    is_new = idxc_ref[...] == past_len
    qb = q_ref[...].astype(BF16)
    rows = []
    for h in range(n_heads):
        kh = jnp.where(is_new, knew_ref[h:h + 1, :], kbuf[slot, :, h, :])
        rows.append(_dot_nt(qb, kh.astype(BF16))[h:h + 1, :])
    logit = jnp.concatenate(rows, axis=0) * (dh ** -0.5) + bias
    logit = jnp.where(dist >= 0, logit, NEG_INF)
    p = jnp.exp(logit - jnp.max(logit, axis=1, keepdims=True))
    pb = (p / jnp.sum(p, axis=1, keepdims=True)).astype(BF16)
    for h in range(n_heads):
        vh = jnp.where(is_new, vnew_ref[h:h + 1, :], vbuf[slot, :, h, :])
        o_ref[h:h + 1, :] = _dot(pb, vh.astype(BF16))[h:h + 1, :]


def _sample_attention(idx3, page_table, q3, rel_bias, knew3, vnew3, cache_k, cache_v, past_len):
    s, kk, _ = idx3.shape
    n_heads, dh = q3.shape[1:]
    page = cache_k.shape[1]
    assert page & (page - 1) == 0
    n_buckets = rel_bias.shape[0]
    rbt = rel_bias.T
    rbh = rbt.astype(BF16)
    rbl = (rbt - rbh.astype(F32)).astype(BF16)
    seq3 = lambda: pl.BlockSpec((None, n_heads, dh), lambda i, a, b: (i, 0, 0))
    grid_spec = pltpu.PrefetchScalarGridSpec(
        num_scalar_prefetch=2,
        grid=(s,),
        in_specs=[seq3(),
                  pl.BlockSpec((None, kk, 1), lambda i, a, b: (i, 0, 0)),
                  pl.BlockSpec((None, 1, kk), lambda i, a, b: (i, 0, 0)),
                  pl.BlockSpec((n_heads, n_buckets), lambda i, a, b: (0, 0)),
                  pl.BlockSpec((n_heads, n_buckets), lambda i, a, b: (0, 0)),
                  seq3(), seq3(),
                  pl.BlockSpec(memory_space=pl.ANY), pl.BlockSpec(memory_space=pl.ANY)],
        out_specs=seq3(),
        scratch_shapes=[pltpu.VMEM((2, kk, n_heads, dh), F32), pltpu.VMEM((2, kk, n_heads, dh), F32),
                        pltpu.SemaphoreType.DMA((2, 2))],
    )
    return pl.pallas_call(
        functools.partial(_sattn_kernel, kk=kk, past_len=past_len, page=page, n_buckets=n_buckets),
        out_shape=_sds((s, n_heads, dh), F32),
        grid_spec=grid_spec,
        compiler_params=_cp(("arbitrary",)),
        name="sample_attention",
    )(idx3.reshape(s, kk), page_table, q3, idx3, idx3.reshape(s, 1, kk), rbh, rbl, knew3, vnew3, cache_k, cache_v)


def kernel(x_prompt, x_sample, cache_k, cache_v, cache_kidx, page_table, state_lru_h, state_lru_conv, state_ffn_conv,
           c_prompt, c_sample, w_ada, b_ada, w_in, lru_conv_w, lru_conv_b, lru_w_a, lru_b_a, lru_w_x, lru_b_x,
           lru_lambda, attn_rel_bias, lru_out_g, attn_out_g, w_out, ln1_g, ln1_b, w_ffn_in, ffn_conv_w, ffn_conv_b,
           w_ffn_out, ln2_g, ln2_b):
    B, T, D = x_prompt.shape
    S, ts, _ = x_sample.shape
    assert ts == 1
    _, page, n_heads, dh = cache_k.shape
    idim = cache_kidx.shape[-1]
    n_pages = page_table.shape[1]
    past_len = n_pages * page
    W = lru_conv_b.shape[0]
    aw = n_heads * dh
    n_idx_heads = (w_in.shape[1] - 2 * W - 3 * aw - idim) // (idim + 1)
    nqi = n_idx_heads * idim
    dff = ffn_conv_b.shape[0]
    assert W % LANES == 0 and dh == LANES and 2 * idim == LANES and S % 8 == 0

    w_in_t = w_in.T
    w_in_b = w_in_t.astype(BF16)
    c0 = 2 * W + 3 * aw
    w_idx = jnp.concatenate([w_in_t[c0:c0 + nqi], w_in_t[c0 + nqi + n_idx_heads:],
                             w_in_t[c0 + nqi:c0 + nqi + n_idx_heads],
                             jnp.zeros((LANES - idim - n_idx_heads, D), F32)], axis=0)
    w_idx_hi = w_idx.astype(BF16)
    w_idx_lo = (w_idx - w_idx_hi.astype(F32)).astype(BF16)
    w_out_b = w_out.astype(BF16)
    w_ffn_in_b = w_ffn_out_b = None
    lru_p = dict(cw=lru_conv_w, cb=lru_conv_b.reshape(1, W),
                 wax=jnp.concatenate([lru_w_a, lru_w_x], axis=2).astype(BF16),
                 ba=lru_b_a.reshape(1, W), bx=lru_b_x.reshape(1, W), lam=lru_lambda.reshape(1, W),
                 g=lru_out_g.reshape(1, W))
    ln1 = (ln1_g.reshape(1, D), ln1_b.reshape(1, D))
    ln2 = (ln2_g.reshape(1, D), ln2_b.reshape(1, D))
    fcw, fcb = ffn_conv_w, ffn_conv_b.reshape(1, dff)
    attn_g = attn_out_g.reshape(1, aw)

    mp = -(-(S + B) // 8) * 8
    c_all = jnp.concatenate([c_sample, c_prompt, jnp.zeros((mp - S - B, D), F32)], axis=0)
    mod = _ada(c_all, w_ada, b_ada)

    st_p = _Stream(B, T, min(1024, T), False, S)
    st_s = _Stream(1, S, S, True, 0)
    outs = {}
    for name, st, x3 in (("p", st_p, x_prompt), ("s", st_s, x_sample.reshape(1, S, D))):
        M = st.M
        m1 = _modulate(st, x3, mod, 0, 1).reshape(M, D)
        xg, = _matmul_nt(m1, w_in_b, 0, 2 * W, [F32], st.tm)
        q_f, q_b = _matmul_nt(m1, w_in_b, 2 * W, aw, [F32, BF16], st.tm)
        k_f, k_b = _matmul_nt(m1, w_in_b, 2 * W + aw, aw, [F32, BF16], st.tm)
        v_f, v_b = _matmul_nt(m1, w_in_b, 2 * W + 2 * aw, aw, [F32, BF16], st.tm)
        qi, wk = _idx_project(st, x3, mod, w_idx_hi, w_idx_lo, nqi)
        qi = qi.reshape(M, nqi)
        wk = wk.reshape(M, LANES)
        kidx = wk[:, :idim]
        if name == "p":
            kk = min(TOPK_MAX, T // 4)
            y_lru, h_last = _lru_prompt(xg, B, T, jnp.zeros((B, 8, W), F32), jnp.zeros((B, 1, W), F32), lru_p)
            conv_state = xg.reshape(B, T, 2 * W)[:, T - 3:, :W]
            mask = _prompt_index(qi, wk, B, T, n_idx_heads, idim, kk)
            y_att, w_ffn_in_b = _prompt_attention(q_b, k_b, v_b, mask, attn_rel_bias, B, T, n_heads, dh, w_ffn_in)
        else:
            kk = min(TOPK_MAX, (past_len + 1) // 4)
            y_lru, h_last = _lru_sample(xg, jnp.swapaxes(state_lru_conv, 0, 1), state_lru_h, lru_p)
            conv_state = jnp.concatenate([state_lru_conv[:, 1:], xg[:, None, :W]], axis=1)
            scale = (n_idx_heads ** -0.5) * (idim ** -0.5)
            scores = _sample_scores(page_table, qi.reshape(S, n_idx_heads, idim),
                                    wk[:, idim:idim + n_idx_heads].reshape(S, n_idx_heads, 1),
                                    kidx.reshape(S, 1, idim), jnp.swapaxes(cache_kidx, 1, 2), scale)
            idx3 = _sample_select(scores.reshape(S, past_len + LANES), kk, past_len + 1)
            y_att = _sample_attention(idx3, page_table, q_f.reshape(S, n_heads, dh), attn_rel_bias,
                                      k_f.reshape(S, n_heads, dh), v_f.reshape(S, n_heads, dh),
                                      cache_k, cache_v, past_len).reshape(S, aw)
        y_att_n = _rms_norm(y_att, attn_g, 512)
        mix = _matmul_cat(y_lru, y_att_n, w_out_b, st.tm)
        x1, m2 = _residual_ln(st, x3, mix.reshape(x3.shape), mod, 2, *ln1, mod_next=(3, 4))
        m2 = m2.reshape(M, D)
        if name == "p":
            hmid, tail, w_ffn_out_b = _ffn_in_prompt(m2, w_ffn_in_b, jnp.zeros((B, 8, dff), F32), fcw, fcb, B, T,
                                                     w_ffn_out)
            ffn_state = tail[:, 6:, :]
        else:
            hmid, gate = _ffn_in_sample(m2, w_ffn_in_b, jnp.swapaxes(state_ffn_conv, 0, 1), fcw, fcb)
            ffn_state = jnp.concatenate([state_ffn_conv[:, 1:], gate[:, None, :]], axis=1)
        f = _matmul_ktiled(hmid, w_ffn_out_b, st.tm, 2048, 2048)
        y = _residual_ln(st, x1, f.reshape(x3.shape), mod, 5, *ln2)
        outs[name] = (y, k_f, v_f, kidx, h_last, conv_state, ffn_state)

    yp, kp, vp, kip, hp_, cp_, fp = outs["p"]
    ys, ks, vs, kis, hs_, cs_, fs = outs["s"]
    return (yp, ys.reshape(S, 1, D),
            kp.reshape(B, T, n_heads, dh), vp.reshape(B, T, n_heads, dh), kip.reshape(B, T, idim), hp_, cp_, fp,
            ks.reshape(S, 1, n_heads, dh), vs.reshape(S, 1, n_heads, dh), kis.reshape(S, 1, idim), hs_, cs_, fs)
```

```python
import functools
import math

import numpy as np
import jax
import jax.numpy as jnp
from jax import lax
from jax.experimental import pallas as pl
from jax.experimental.pallas import tpu as pltpu

F32 = jnp.float32
BF16 = jnp.bfloat16
I32 = jnp.int32

LRU_C = 8.0
TOPK_MAX = 256
MAX_DISTANCE = 128
LN_EPS = 1e-5
DEPTH = 1
DEEPNORM_ALPHA = (2.0 * DEPTH) ** 0.25
LANES = 128
VMEM_LIMIT = 56 * 1024 * 1024
INT_MIN = -(2 ** 31)
NEG_INF = float("-inf")


def _cp(sem, vmem=VMEM_LIMIT):
    return pltpu.CompilerParams(dimension_semantics=sem, vmem_limit_bytes=vmem)


def _sds(shape, dtype):
    return jax.ShapeDtypeStruct(shape, dtype)


def _split_bf16(x):
    hi = x.astype(BF16)
    lo = (x - hi.astype(F32)).astype(BF16)
    return hi, lo


def _dot(a, b):
    return jnp.dot(a, b, preferred_element_type=F32)


def _dot_nt(a, b):
    return lax.dot_general(a, b, (((1,), (1,)), ((), ())), preferred_element_type=F32)


def _dot3(a, b):
    ah, al = _split_bf16(a)
    bh, bl = _split_bf16(b)
    return _dot(ah, bh) + (_dot(al, bh) + _dot(ah, bl))


def _sortable(x):
    b = pltpu.bitcast(x, I32)
    return b ^ ((b >> 31) & 0x7FFFFFFF)


class _Stream:
    def __init__(self, G, R, tm, per_row, mod_row0):
        self.G, self.R, self.tm, self.per_row, self.mod_row0 = G, R, tm, per_row, mod_row0
        self.M = G * R
        self.nr = R // tm

    def mod_operand(self, mod2):
        if self.per_row:
            return mod2
        mp, n6 = mod2.shape
        return mod2.reshape(mp, 6, 1, n6 // 6)

    def mod_spec(self, which, width, d_model, col=lambda *ids: 0):
        if self.per_row:
            nb = d_model // width
            return pl.BlockSpec((self.tm, width), lambda g, i, *r: (i, which * nb + col(g, i, *r)))
        r0 = self.mod_row0
        return pl.BlockSpec((None, None, 1, width), lambda g, i, *r: (r0 + g, which, 0, col(g, i, *r)))


def _ada_kernel(c_ref, w_ref, b_ref, o_ref):
    c = c_ref[...]
    a = c * jax.nn.sigmoid(c)
    o_ref[...] = _dot3(a, w_ref[...]) + b_ref[...]


def _ada(c_all, w_ada, b_ada):
    mp, d = c_all.shape
    n = w_ada.shape[1]
    tn = 512
    return pl.pallas_call(
        _ada_kernel,
        out_shape=_sds((mp, n), F32),
        grid=(n // tn,),
        in_specs=[pl.BlockSpec((mp, d), lambda j: (0, 0)),
                  pl.BlockSpec((d, tn), lambda j: (0, j)),
                  pl.BlockSpec((1, tn), lambda j: (0, j))],
        out_specs=pl.BlockSpec((mp, tn), lambda j: (0, j)),
        compiler_params=_cp(("arbitrary",)),
        name="ada_mod",
    )(c_all, w_ada, b_ada.reshape(1, n))


def _mod_kernel(x_ref, sh_ref, sc_ref, o_ref):
    o_ref[...] = (x_ref[...] * (1.0 + sc_ref[...]) + sh_ref[...]).astype(o_ref.dtype)


def _modulate(st, x3, mod2, sh_which, sc_which):
    d = x3.shape[-1]
    tm = min(st.tm, 512)
    st2 = _Stream(st.G, st.R, tm, st.per_row, st.mod_row0)
    modop = st2.mod_operand(mod2)
    return pl.pallas_call(
        _mod_kernel,
        out_shape=_sds((st.G, st.R, d), BF16),
        grid=(st.G, st.R // tm),
        in_specs=[pl.BlockSpec((None, tm, d), lambda g, i: (g, i, 0)),
                  st2.mod_spec(sh_which, d, d), st2.mod_spec(sc_which, d, d)],
        out_specs=pl.BlockSpec((None, tm, d), lambda g, i: (g, i, 0)),
        compiler_params=_cp(("arbitrary", "arbitrary")),
        name="modulate",
    )(x3, modop, modop)


def _mm_nt_kernel(a_ref, bt_ref, *o_refs):
    r = _dot_nt(a_ref[...], bt_ref[...])
    for o in o_refs:
        o[...] = r.astype(o.dtype)


def _matmul_nt(a, bt, col0, n, out_dtypes, tm, tn=1024):
    m, k = a.shape
    tm = min(tm, m)
    tn = min(tn, n)
    assert m % tm == 0 and n % tn == 0 and col0 % tn == 0
    cb = col0 // tn
    outs = pl.pallas_call(
        _mm_nt_kernel,
        out_shape=[_sds((m, n), dt) for dt in out_dtypes],
        grid=(m // tm, n // tn),
        in_specs=[pl.BlockSpec((tm, k), lambda i, j: (i, 0)),
                  pl.BlockSpec((tn, k), lambda i, j: (cb + j, 0))],
        out_specs=[pl.BlockSpec((tm, tn), lambda i, j: (i, j)) for _ in out_dtypes],
        compiler_params=_cp(("arbitrary", "arbitrary")),
        name="matmul_nt",
    )(a, bt)
    return outs


def _mm2_kernel(a1_ref, a2_ref, b_ref, o_ref):
    k1 = a1_ref.shape[1]
    o_ref[...] = _dot(a1_ref[...], b_ref[0:k1, :]) + _dot(a2_ref[...], b_ref[k1:, :])


def _matmul_cat(a1, a2, b, tm, tn=1024):
    m, k1 = a1.shape
    k2 = a2.shape[1]
    n = b.shape[1]
    tm = min(tm, m)
    return pl.pallas_call(
        _mm2_kernel,
        out_shape=_sds((m, n), F32),
        grid=(m // tm, n // tn),
        in_specs=[pl.BlockSpec((tm, k1), lambda i, j: (i, 0)),
                  pl.BlockSpec((tm, k2), lambda i, j: (i, 0)),
                  pl.BlockSpec((k1 + k2, tn), lambda i, j: (0, j))],
        out_specs=pl.BlockSpec((tm, tn), lambda i, j: (i, j)),
        compiler_params=_cp(("arbitrary", "arbitrary")),
        name="matmul_cat",
    )(a1, a2, b)


def _mmk_kernel(a_ref, b_ref, o_ref):
    part = _dot(a_ref[...], b_ref[...])

    @pl.when(pl.program_id(2) == 0)
    def _():
        o_ref[...] = part

    @pl.when(pl.program_id(2) > 0)
    def _():
        o_ref[...] += part


def _matmul_ktiled(a, b, tm, tn, tk):
    m, k = a.shape
    n = b.shape[1]
    tm = min(tm, m)
    return pl.pallas_call(
        _mmk_kernel,
        out_shape=_sds((m, n), F32),
        grid=(m // tm, n // tn, k // tk),
        in_specs=[pl.BlockSpec((tm, tk), lambda i, j, kk: (i, kk)),
                  pl.BlockSpec((tk, tn), lambda i, j, kk: (kk, j))],
        out_specs=pl.BlockSpec((tm, tn), lambda i, j, kk: (i, j)),
        compiler_params=_cp(("arbitrary", "arbitrary", "arbitrary")),
        name="matmul_ktiled",
    )(a, b)


def _idxproj_kernel(x_ref, sh_ref, sc_ref, wh_ref, wl_ref, q_ref, wk_ref, acc_ref, *, nq):
    kk = pl.program_id(2)
    m = x_ref[...] * (1.0 + sc_ref[...]) + sh_ref[...]
    mh, ml = _split_bf16(m)
    wh = wh_ref[...]
    part = _dot_nt(mh, wh) + (_dot_nt(ml, wh) + _dot_nt(mh, wl_ref[...]))

    @pl.when(kk == 0)
    def _():
        acc_ref[...] = part

    @pl.when(kk > 0)
    def _():
        acc_ref[...] += part

    @pl.when(kk == pl.num_programs(2) - 1)
    def _():
        q_ref[...] = acc_ref[:, 0:nq]
        wk_ref[...] = acc_ref[:, nq:]


def _idx_project(st, x3, mod2, w_hi, w_lo, nq):
    d = x3.shape[-1]
    nw = w_hi.shape[0]
    tm = min(st.tm, 512)
    tk = 1024
    st2 = _Stream(st.G, st.R, tm, st.per_row, st.mod_row0)
    modop = st2.mod_operand(mod2)
    kcol = lambda g, i, kk: kk
    q, wk = pl.pallas_call(
        functools.partial(_idxproj_kernel, nq=nq),
        out_shape=[_sds((st.G, st.R, nq), F32), _sds((st.G, st.R, nw - nq), F32)],
        grid=(st.G, st.R // tm, d // tk),
        in_specs=[pl.BlockSpec((None, tm, tk), lambda g, i, kk: (g, i, kk)),
                  st2.mod_spec(0, tk, d, kcol), st2.mod_spec(1, tk, d, kcol),
                  pl.BlockSpec((nw, tk), lambda g, i, kk: (0, kk)),
                  pl.BlockSpec((nw, tk), lambda g, i, kk: (0, kk))],
        out_specs=[pl.BlockSpec((None, tm, nq), lambda g, i, kk: (g, i, 0)),
                   pl.BlockSpec((None, tm, nw - nq), lambda g, i, kk: (g, i, 0))],
        scratch_shapes=[pltpu.VMEM((tm, nw), F32)],
        compiler_params=_cp(("arbitrary", "arbitrary", "arbitrary")),
        name="idx_project",
    )(x3, modop, modop, w_hi, w_lo)
    return q, wk


def _softplus(z):
    return jnp.maximum(z, 0.0) + jnp.log1p(jnp.exp(-jnp.abs(z)))


def _neg_expm1(x):
    poly = x * (-1.0 + x * (-1.0 / 2 + x * (-1.0 / 6 + x * (-1.0 / 24))))
    return jnp.where(x > -1.0 / 16, poly, 1.0 - jnp.exp(x))


def _lru_gates(xc, wax, ba, bx, lam):
    ri = _dot(xc.astype(BF16), wax)
    r = jax.nn.sigmoid(ri[:, :LANES] + ba)
    ig = jax.nn.sigmoid(ri[:, LANES:] + bx)
    log_a = (-LRU_C * r) * _softplus(-lam)
    a = jnp.exp(log_a)
    u = jnp.sqrt(_neg_expm1(2.0 * log_a)) * (ig * xc)
    return a, u


def _lru_prompt_kernel(xr_ref, gr_ref, hist_ref, h0_ref, cw_ref, cb_ref, wax_ref, ba_ref, bx_ref, lam_ref, g_ref,
                       wsrc_ref, y_ref, hlast_ref, wdst_ref, xbuf, hcar, ybuf):
    wdst_ref[...] = wsrc_ref[...].astype(wdst_ref.dtype)
    t = pl.program_id(1)
    tc, w = xr_ref.shape
    nb = w // LANES

    @pl.when(t == 0)
    def _():
        xbuf[0:8, :] = hist_ref[...]
        hcar[...] = h0_ref[...]

    @pl.when(t > 0)
    def _():
        xbuf[0:8, :] = xbuf[tc:tc + 8, :]

    xbuf[8:8 + tc, :] = xr_ref[...]
    rows3 = lax.broadcasted_iota(I32, (tc // 8, 8, LANES), 1)
    ssq = jnp.zeros((tc, 1), F32)
    for n in range(nb):
        sl = slice(n * LANES, (n + 1) * LANES)
        z3 = xbuf[0:8 + tc, sl].reshape(tc // 8 + 1, 8, LANES)
        xc = cb_ref[:, sl]
        for d in (3, 2, 1):
            rot = pltpu.roll(z3, d, axis=1)
            xc = xc + jnp.where(rows3 >= d, rot[1:], rot[:-1]).reshape(tc, LANES) * cw_ref[3 - d:4 - d, sl]
        xc = xc + z3[1:].reshape(tc, LANES) * cw_ref[3:4, sl]
        a, u = _lru_gates(xc, wax_ref[n], ba_ref[:, sl], bx_ref[:, sl], lam_ref[:, sl])
        a3 = a.reshape(tc // 8, 8, LANES)
        u3 = u.reshape(tc // 8, 8, LANES)
        for s in (1, 2, 4):
            keep = rows3 >= s
            u3 = jnp.where(keep, a3 * pltpu.roll(u3, s, axis=1) + u3, u3)
            a3 = jnp.where(keep, a3 * pltpu.roll(a3, s, axis=1), a3)
        h_in = hcar[:, sl]
        hs = []
        for g in range(tc // 8):
            hg = a3[g] * h_in + u3[g]
            hs.append(hg)
            h_in = hg[7:8, :]
        h = jnp.concatenate(hs, axis=0)
        hcar[:, sl] = h_in
        y = h * jax.nn.gelu(gr_ref[:, sl])
        ybuf[:, sl] = y
        ssq = ssq + jnp.sum(y * y, axis=1, keepdims=True)
    scale = lax.rsqrt(ssq * (1.0 / w) + LN_EPS)
    y_ref[...] = (ybuf[...] * scale * g_ref[...]).astype(y_ref.dtype)
    hlast_ref[...] = hcar[...]


def _lru_prompt(xg, B, T, hist8, h0, p, w_cast):
    w = xg.shape[1] // 2
    tc = min(256, T)
    nt = T // tc
    rc = _rider_rows(w_cast.shape[0], B * nt)
    wn = w_cast.shape[1]
    vec = lambda: pl.BlockSpec((1, w), lambda b, t: (0, 0))
    y, hl, w_b = pl.pallas_call(
        _lru_prompt_kernel,
        out_shape=[_sds((B * T, w), BF16), _sds((B, 1, w), F32), _sds(w_cast.shape, BF16)],
        grid=(B, nt),
        in_specs=[pl.BlockSpec((tc, w), lambda b, t: (b * nt + t, 0)),
                  pl.BlockSpec((tc, w), lambda b, t: (b * nt + t, 1)),
                  pl.BlockSpec((None, 8, w), lambda b, t: (b, 0, 0)),
                  pl.BlockSpec((None, 1, w), lambda b, t: (b, 0, 0)),
                  pl.BlockSpec((4, w), lambda b, t: (0, 0)), vec(),
                  pl.BlockSpec(p["wax"].shape, lambda b, t: (0, 0, 0)),
                  vec(), vec(), vec(), vec(),
                  pl.BlockSpec((rc, wn), lambda b, t: (b * nt + t, 0))],
        out_specs=[pl.BlockSpec((tc, w), lambda b, t: (b * nt + t, 0)),
                   pl.BlockSpec((None, 1, w), lambda b, t: (b, 0, 0)),
                   pl.BlockSpec((rc, wn), lambda b, t: (b * nt + t, 0))],
        scratch_shapes=[pltpu.VMEM((tc + 8, w), F32), pltpu.VMEM((1, w), F32), pltpu.VMEM((tc, w), F32)],
        compiler_params=_cp(("arbitrary", "arbitrary")),
        name="rglru_prompt",
    )(xg, xg, hist8, h0, p["cw"], p["cb"], p["wax"], p["ba"], p["bx"], p["lam"], p["g"], w_cast)
    return y, hl.reshape(B, w), w_b


def _lru_sample_kernel(xr_ref, gr_ref, hist_ref, h0_ref, cw_ref, cb_ref, wax_ref, ba_ref, bx_ref, lam_ref, g_ref,
                       y_ref, h_ref, ybuf):
    r, w = xr_ref.shape
    nb = w // LANES
    ssq = jnp.zeros((r, 1), F32)
    for n in range(nb):
        sl = slice(n * LANES, (n + 1) * LANES)
        xc = cb_ref[:, sl]
        for j in range(3):
            xc = xc + hist_ref[j, :, sl] * cw_ref[j:j + 1, sl]
        xc = xc + xr_ref[:, sl] * cw_ref[3:4, sl]
        a, u = _lru_gates(xc, wax_ref[n], ba_ref[:, sl], bx_ref[:, sl], lam_ref[:, sl])
        h = a * h0_ref[:, sl] + u
        h_ref[:, sl] = h
        y = h * jax.nn.gelu(gr_ref[:, sl])
        ybuf[:, sl] = y
        ssq = ssq + jnp.sum(y * y, axis=1, keepdims=True)
    scale = lax.rsqrt(ssq * (1.0 / w) + LN_EPS)
    y_ref[...] = (ybuf[...] * scale * g_ref[...]).astype(y_ref.dtype)


def _lru_sample(xg, hist_t, h0, p):
    s = xg.shape[0]
    w = xg.shape[1] // 2
    vec = lambda: pl.BlockSpec((1, w), lambda i: (0, 0))
    return pl.pallas_call(
        _lru_sample_kernel,
        out_shape=[_sds((s, w), BF16), _sds((s, w), F32)],
        grid=(1,),
        in_specs=[pl.BlockSpec((s, w), lambda i: (0, 0)), pl.BlockSpec((s, w), lambda i: (0, 1)),
                  pl.BlockSpec((3, s, w), lambda i: (0, 0, 0)), pl.BlockSpec((s, w), lambda i: (0, 0)),
                  pl.BlockSpec((4, w), lambda i: (0, 0)), vec(),
                  pl.BlockSpec(p["wax"].shape, lambda i: (0, 0, 0)),
                  vec(), vec(), vec(), vec()],
        out_specs=[pl.BlockSpec((s, w), lambda i: (0, 0)), pl.BlockSpec((s, w), lambda i: (0, 0))],
        scratch_shapes=[pltpu.VMEM((s, w), F32)],
        compiler_params=_cp(("arbitrary",)),
        name="rglru_sample",
    )(xg, xg, hist_t, h0, p["cw"], p["cb"], p["wax"], p["ba"], p["bx"], p["lam"], p["g"])


def _kth_largest(count_ge, kk, rows):
    def body(p, thr):
        bit = jnp.left_shift(jnp.int32(1), 31 - p)
        trial = thr + bit
        return jnp.where(count_ge(trial) >= kk, trial, thr)
    return lax.fori_loop(0, 32, body, jnp.full((rows, 1), INT_MIN, I32))


def _pidx_kernel(q_ref, wkq_ref, wkall_ref, o_ref, kcat, qcat, wb, keys, *, n_heads, idim, kk, kc):
    i = pl.program_id(1)
    tq = q_ref.shape[0]
    t_all = wkall_ref.shape[0]
    lane = lax.broadcasted_iota(I32, (tq, LANES), 1)

    @pl.when(i == 0)
    def _():
        k = wkall_ref[...]
        kh = k.astype(BF16).astype(F32)
        kl = k - kh
        left = lax.broadcasted_iota(I32, k.shape, 1) < idim
        kcat[:, 0:LANES] = jnp.where(left, kh, pltpu.roll(kl, idim, axis=1)).astype(BF16)
        kcat[:, LANES:] = jnp.where(left, kh, 0.0).astype(BF16)

    for pr in range(n_heads // 2):
        v = q_ref[:, pr * LANES:(pr + 1) * LANES]
        vh = v.astype(BF16).astype(F32)
        vl = v - vh
        vh_r = pltpu.roll(vh, idim, axis=1)
        vl_r = pltpu.roll(vl, idim, axis=1)
        first = lane < idim
        qcat[2 * pr, :, 0:LANES] = jnp.where(first, vh, vh_r).astype(BF16)
        qcat[2 * pr, :, LANES:] = jnp.where(first, vl, 0.0).astype(BF16)
        qcat[2 * pr + 1, :, 0:LANES] = jnp.where(first, vh_r, vh).astype(BF16)
        qcat[2 * pr + 1, :, LANES:] = jnp.where(first, vl_r, 0.0).astype(BF16)
    scale = (n_heads ** -0.5) * (idim ** -0.5)
    wq = wkq_ref[...] * scale
    for h in range(n_heads):
        wb[h] = jnp.broadcast_to(wq[:, idim + h:idim + h + 1], (tq, LANES))

    nch = (i * tq + tq + kc - 1) // kc
    qpos = i * tq + lax.broadcasted_iota(I32, (tq, kc), 0)
    cols = lax.broadcasted_iota(I32, (tq, kc), 1)

    def score_chunk(c, carry):
        k_c = kcat[pl.ds(pl.multiple_of(c * kc, kc), kc), :]
        x_all = _dot_nt(qcat[...].reshape(n_heads * tq, 2 * LANES), k_c)
        acc = jnp.zeros((tq, kc), F32)
        for h in range(n_heads):
            acc = acc + jnp.maximum(x_all[h * tq:(h + 1) * tq], 0.0) * jnp.tile(wb[h], (1, kc // LANES))
        key = jnp.where(c * kc + cols <= qpos, _sortable(acc + 0.0), INT_MIN)
        keys[:, pl.ds(pl.multiple_of(c * kc, kc), kc)] = key
        return carry

    lax.fori_loop(0, nch, score_chunk, 0)

    def count_where(preds):
        def body(c, cnts):
            kv = keys[:, pl.ds(pl.multiple_of(c * kc, kc), kc)]
            return [cnt + _fold_lanes(pred(kv).astype(I32), jnp.add) for cnt, pred in zip(cnts, preds)]
        cnts = lax.fori_loop(0, nch, body, [jnp.zeros((tq, LANES), I32) for _ in preds])
        return [jnp.sum(cnt, axis=1, keepdims=True) for cnt in cnts]

    thr = _kth_largest(lambda t: count_where([lambda kv: kv >= t])[0], kk, tq)
    n_ge, = count_where([lambda kv: (kv >= thr) & (kv > INT_MIN)])
    tie = jnp.max(n_ge) > kk

    o_ref[...] = jnp.full(o_ref.shape, NEG_INF, F32)

    @pl.when(jnp.logical_not(tie))
    def _():
        def body(c, carry):
            sl = pl.ds(pl.multiple_of(c * kc, kc), kc)
            kv = keys[:, sl]
            o_ref[:, sl] = jnp.where((kv >= thr) & (kv > INT_MIN), 0.0, NEG_INF)
            return carry
        lax.fori_loop(0, nch, body, 0)

    @pl.when(tie)
    def _():
        n_gt, = count_where([lambda kv: kv > thr])
        need = (kk - n_gt).astype(F32)
        tri = (lax.broadcasted_iota(I32, (kc, kc), 0) < lax.broadcasted_iota(I32, (kc, kc), 1)).astype(BF16)

        def body(c, seen):
            sl = pl.ds(pl.multiple_of(c * kc, kc), kc)
            kv = keys[:, sl]
            eq = (kv == thr) & (kv > INT_MIN)
            eqf = jnp.where(eq, 1.0, 0.0)
            before = seen + _dot(eqf.astype(BF16), tri)
            sel = (kv > thr) | (eq & (before < need))
            o_ref[:, sl] = jnp.where(sel, 0.0, NEG_INF)
            return seen + jnp.sum(eqf, axis=1, keepdims=True)
        lax.fori_loop(0, nch, body, jnp.zeros((tq, 1), F32))


def _prompt_index(qi, wk, B, T, n_heads, idim, kk):
    tq = min(256, T)
    kc = min(512, T)
    nq = T // tq
    return pl.pallas_call(
        functools.partial(_pidx_kernel, n_heads=n_heads, idim=idim, kk=kk, kc=kc),
        out_shape=_sds((B * T, T), F32),
        grid=(B, nq),
        in_specs=[pl.BlockSpec((tq, n_heads * idim), lambda b, i: (b * nq + i, 0)),
                  pl.BlockSpec((tq, LANES), lambda b, i: (b * nq + i, 0)),
                  pl.BlockSpec((T, LANES), lambda b, i: (b, 0))],
        out_specs=pl.BlockSpec((tq, T), lambda b, i: (b * nq + i, 0)),
        scratch_shapes=[pltpu.VMEM((T, 2 * LANES), BF16), pltpu.VMEM((n_heads, tq, 2 * LANES), BF16),
                        pltpu.VMEM((n_heads, tq, LANES), F32), pltpu.VMEM((tq, T), I32)],
        compiler_params=_cp(("arbitrary", "arbitrary")),
        name="prompt_index",
    )(qi, wk, wk)


def _bucket_np(d):
    n_buckets = 32
    max_exact = n_buckets // 2
    d = np.maximum(d, 0)
    large = max_exact + (np.log(np.maximum(d, 1).astype(np.float32) / np.float32(max_exact))
                         / np.float32(math.log(MAX_DISTANCE / max_exact))
                         * np.float32(n_buckets - max_exact)).astype(np.int32)
    large = np.minimum(large, n_buckets - 1)
    return np.where(d < max_exact, d, large).astype(np.int32)


def _bucket_jnp(d, n_buckets):
    max_exact = n_buckets // 2
    d = jnp.maximum(d, 0)
    large = max_exact + (jnp.log(jnp.maximum(d, 1).astype(F32) / max_exact)
                         / math.log(MAX_DISTANCE / max_exact) * (n_buckets - max_exact)).astype(I32)
    large = jnp.minimum(large, n_buckets - 1)
    return jnp.where(d < max_exact, d, large)


def _fold_lanes(x, op):
    out = x[:, 0:LANES]
    for s in range(1, x.shape[1] // LANES):
        out = op(out, x[:, s * LANES:(s + 1) * LANES])
    return out


def _pattn_kernel(rb_ref, q_ref, k_ref, v_ref, mask_ref, bkt_ref, wsrc_ref, o_ref, wdst_ref, tbl, sbuf, mx_s, l_s, acc_s,
                  *, hp, kc, n_far):
    wdst_ref[...] = wsrc_ref[...].astype(wdst_ref.dtype)
    hg = pl.program_id(1)
    i = pl.program_id(2)
    tq = q_ref.shape[0]
    dh = q_ref.shape[1] // hp
    log2e = math.log2(math.e)
    c_exp = dh ** -0.5 * log2e
    n_buckets = rb_ref.shape[0]

    @pl.when(i == 0)
    def _():
        tbl[...] = jnp.zeros(tbl.shape, F32)
        bkt = bkt_ref[:, kc - tq:kc + tq]
        for h in range(hp):
            far_b = rb_ref[n_far, hg * hp + h]
            acc = jnp.full(bkt.shape, far_b, F32)
            for b in range(n_buckets):
                acc = jnp.where(bkt == b, rb_ref[b, hg * hp + h], acc)
            tbl[h, :, kc - tq:kc + tq] = (acc - far_b) * log2e

    for h in range(hp):
        mx_s[h] = jnp.full((tq, LANES), NEG_INF, F32)
        l_s[h] = jnp.zeros((tq, LANES), F32)
        acc_s[h] = jnp.zeros((tq, dh), F32)

    nch = (i * tq + tq + kc - 1) // kc
    c_near = jnp.maximum((i * tq - tq) // kc, 0)

    def logits(c, bias_of):
        sl = pl.ds(pl.multiple_of(c * kc, kc), kc)
        msk = mask_ref[:, sl]
        for h in range(hp):
            hs = slice(h * dh, (h + 1) * dh)
            s = bias_of(h, _dot_nt(q_ref[:, hs], k_ref[sl, hs]) * c_exp + msk)
            sbuf[h, :, sl] = s
            mx_s[h] = jnp.maximum(mx_s[h], _fold_lanes(s, jnp.maximum))

    def far(c, carry):
        logits(c, lambda h, s: s)
        return carry

    def near(c, carry):
        off = pl.multiple_of(kc - (i * tq - c * kc), LANES)
        logits(c, lambda h, s: s + tbl[h, :, pl.ds(off, kc)])
        return carry

    lax.fori_loop(0, c_near, far, 0)
    lax.fori_loop(c_near, nch, near, 0)
    for h in range(hp):
        mx_s[h] = jnp.broadcast_to(jnp.max(mx_s[h], axis=1, keepdims=True), (tq, LANES))

    ones = jnp.ones((kc, LANES), BF16)

    def weigh(c, carry):
        sl = pl.ds(pl.multiple_of(c * kc, kc), kc)
        for h in range(hp):
            hs = slice(h * dh, (h + 1) * dh)
            p = jnp.exp2(sbuf[h, :, sl] - jnp.tile(mx_s[h], (1, kc // LANES))).astype(BF16)
            r = _dot(p, jnp.concatenate([v_ref[sl, hs], ones], axis=1))
            acc_s[h] = acc_s[h] + r[:, :dh]
            l_s[h] = l_s[h] + r[:, dh:]
        return carry

    lax.fori_loop(0, nch, weigh, 0)
    for h in range(hp):
        o_ref[:, h * dh:(h + 1) * dh] = acc_s[h] / l_s[h]


def _prompt_attention(q, k, v, mask, rel_bias, B, T, n_heads, dh, w_cast):
    tq = 128
    kc = min(1024, T)
    hp = 4
    nq = T // tq
    ng = n_heads // hp
    rc = _rider_rows(w_cast.shape[0], B * ng * nq)
    wn = w_cast.shape[1]
    n_buckets = rel_bias.shape[0]
    r = np.arange(tq)[:, None]
    x = np.arange(2 * kc)[None, :]
    dist = r + kc - x
    bkt = np.where(dist >= 0, _bucket_np(dist), -1).astype(np.int32)
    far_d = kc + tq
    assert _bucket_np(np.array([tq + 1]))[0] == n_buckets - 1 and far_d > tq
    grid_spec = pltpu.PrefetchScalarGridSpec(
        num_scalar_prefetch=0,
        grid=(B, n_heads // hp, nq),
        in_specs=[pl.BlockSpec(memory_space=pltpu.SMEM),
                  pl.BlockSpec((tq, hp * dh), lambda b, g, i: (b * nq + i, g)),
                  pl.BlockSpec((T, hp * dh), lambda b, g, i: (b, g)),
                  pl.BlockSpec((T, hp * dh), lambda b, g, i: (b, g)),
                  pl.BlockSpec((tq, T), lambda b, g, i: (b * nq + i, 0)),
                  pl.BlockSpec((tq, 2 * kc), lambda b, g, i: (0, 0)),
                  pl.BlockSpec((rc, wn), lambda b, g, i: ((b * ng + g) * nq + i, 0))],
        out_specs=[pl.BlockSpec((tq, hp * dh), lambda b, g, i: (b * nq + i, g)),
                   pl.BlockSpec((rc, wn), lambda b, g, i: ((b * ng + g) * nq + i, 0))],
        scratch_shapes=[pltpu.VMEM((hp, tq, 2 * kc), F32), pltpu.VMEM((hp, tq, T), F32),
                        pltpu.VMEM((hp, tq, LANES), F32), pltpu.VMEM((hp, tq, LANES), F32),
                        pltpu.VMEM((hp, tq, dh), F32)],
    )
    return pl.pallas_call(
        functools.partial(_pattn_kernel, hp=hp, kc=kc, n_far=n_buckets - 1),
        out_shape=[_sds((B * T, n_heads * dh), F32), _sds(w_cast.shape, BF16)],
        grid_spec=grid_spec,
        compiler_params=_cp(("arbitrary", "arbitrary", "arbitrary")),
        name="prompt_attention",
    )(rel_bias, q, k, v, mask, jnp.asarray(bkt), w_cast)


def _rms_kernel(x_ref, g_ref, o_ref):
    x = x_ref[...]
    ms = jnp.mean(x * x, axis=-1, keepdims=True)
    o_ref[...] = (x * lax.rsqrt(ms + LN_EPS) * g_ref[...]).astype(o_ref.dtype)


def _rms_norm(x, g, tm):
    m, w = x.shape
    tm = min(tm, m)
    return pl.pallas_call(
        _rms_kernel,
        out_shape=_sds((m, w), BF16),
        grid=(m // tm,),
        in_specs=[pl.BlockSpec((tm, w), lambda i: (i, 0)), pl.BlockSpec((1, w), lambda i: (0, 0))],
        out_specs=pl.BlockSpec((tm, w), lambda i: (i, 0)),
        compiler_params=_cp(("arbitrary",)),
        name="rms_norm",
    )(x, g)


def _layer_norm(z, g, b):
    mu = jnp.mean(z, axis=-1, keepdims=True)
    zc = z - mu
    var = jnp.mean(zc * zc, axis=-1, keepdims=True)
    return zc * lax.rsqrt(var + LN_EPS) * g + b


def _ln_mod_kernel(x_ref, f_ref, gate_ref, sh_ref, sc_ref, g_ref, b_ref, x1_ref, m_ref):
    z = DEEPNORM_ALPHA * x_ref[...] + gate_ref[...] * f_ref[...]
    x1 = _layer_norm(z, g_ref[...], b_ref[...])
    x1_ref[...] = x1
    m_ref[...] = (x1 * (1.0 + sc_ref[...]) + sh_ref[...]).astype(m_ref.dtype)


def _ln_kernel(x_ref, f_ref, gate_ref, g_ref, b_ref, y_ref):
    z = DEEPNORM_ALPHA * x_ref[...] + gate_ref[...] * f_ref[...]
    y_ref[...] = _layer_norm(z, g_ref[...], b_ref[...])


def _residual_ln(st, x3, f3, mod2, gate_which, ln_g, ln_b, mod_next=None):
    d = x3.shape[-1]
    tm = min(st.tm, 256)
    st2 = _Stream(st.G, st.R, tm, st.per_row, st.mod_row0)
    modop = st2.mod_operand(mod2)
    row = pl.BlockSpec((None, tm, d), lambda g, i: (g, i, 0))
    vec = pl.BlockSpec((1, d), lambda g, i: (0, 0))
    if mod_next is None:
        return pl.pallas_call(
            _ln_kernel,
            out_shape=_sds(x3.shape, F32),
            grid=(st.G, st.R // tm),
            in_specs=[row, row, st2.mod_spec(gate_which, d, d), vec, vec],
            out_specs=row,
            compiler_params=_cp(("arbitrary", "arbitrary")),
            name="residual_ln",
        )(x3, f3, modop, ln_g, ln_b)
    return pl.pallas_call(
        _ln_mod_kernel,
        out_shape=[_sds(x3.shape, F32), _sds(x3.shape, BF16)],
        grid=(st.G, st.R // tm),
        in_specs=[row, row, st2.mod_spec(gate_which, d, d), st2.mod_spec(mod_next[0], d, d),
                  st2.mod_spec(mod_next[1], d, d), vec, vec],
        out_specs=[row, row],
        compiler_params=_cp(("arbitrary", "arbitrary")),
        name="residual_ln_mod",
    )(x3, f3, modop, modop, modop, ln_g, ln_b)


def _rider_rows(n_rows, n_steps):
    rows = n_rows // n_steps
    assert rows * n_steps == n_rows and rows % 16 == 0, (n_rows, n_steps)
    return rows


def _ffn_in_seq_kernel(a_ref, wg_ref, wu_ref, hist_ref, cw_ref, cb_ref, wsrc_ref, h_ref, tail_ref, wdst_ref, carry, *,
                       rows_per_seq):
    wdst_ref[...] = wsrc_ref[...].astype(wdst_ref.dtype)
    i = pl.program_id(1)
    tm = a_ref.shape[0]
    tiles_per_seq = rows_per_seq // tm
    tn = wg_ref.shape[1]
    cn = min(tn, 2 * LANES)

    @pl.when(i % tiles_per_seq == 0)
    def _():
        carry[...] = hist_ref[...]

    rows = lax.broadcasted_iota(I32, (tm, cn), 0)
    for c in range(tn // cn):
        cs = slice(c * cn, (c + 1) * cn)
        gate = _dot(a_ref[...], wg_ref[:, cs])
        up = _dot(a_ref[...], wu_ref[:, cs])
        prev = carry[:, cs]
        g1 = jnp.where(rows >= 1, pltpu.roll(gate, 1, axis=0), prev[7:8, :])
        g2 = jnp.where(rows >= 2, pltpu.roll(gate, 2, axis=0), jnp.where(rows == 1, prev[7:8, :], prev[6:7, :]))
        gc = cb_ref[:, cs] + g2 * cw_ref[0:1, cs] + g1 * cw_ref[1:2, cs] + gate * cw_ref[2:3, cs]
        h_ref[:, cs] = (jax.nn.gelu(gc) * up).astype(h_ref.dtype)
        carry[:, cs] = gate[tm - 8:tm, :]
        tail_ref[:, cs] = gate[tm - 8:tm, :]


def _ffn_in_prompt(m2, w_ffn_in, hist8, cw, cb, B, T, w_cast):
    mt, d = m2.shape
    dff = w_ffn_in.shape[1] // 2
    tm = min(1024, T)
    tn = 512
    nj = dff // tn
    ni = mt // tm
    tps = T // tm
    rc = _rider_rows(w_cast.shape[0], nj * ni)
    wn = w_cast.shape[1]
    return pl.pallas_call(
        functools.partial(_ffn_in_seq_kernel, rows_per_seq=T),
        out_shape=[_sds((mt, dff), BF16), _sds((B, 8, dff), F32), _sds(w_cast.shape, BF16)],
        grid=(nj, ni),
        in_specs=[pl.BlockSpec((tm, d), lambda j, i: (i, 0)),
                  pl.BlockSpec((d, tn), lambda j, i: (0, j)),
                  pl.BlockSpec((d, tn), lambda j, i: (0, nj + j)),
                  pl.BlockSpec((None, 8, tn), lambda j, i: (i // tps, 0, j)),
                  pl.BlockSpec((3, tn), lambda j, i: (0, j)),
                  pl.BlockSpec((1, tn), lambda j, i: (0, j)),
                  pl.BlockSpec((rc, wn), lambda j, i: (j * ni + i, 0))],
        out_specs=[pl.BlockSpec((tm, tn), lambda j, i: (i, j)),
                   pl.BlockSpec((None, 8, tn), lambda j, i: (i // tps, 0, j)),
                   pl.BlockSpec((rc, wn), lambda j, i: (j * ni + i, 0))],
        scratch_shapes=[pltpu.VMEM((8, tn), F32)],
        compiler_params=_cp(("arbitrary", "arbitrary")),
        name="ffn_in_prompt",
    )(m2, w_ffn_in, w_ffn_in, hist8, cw, cb, w_cast)


def _ffn_in_row_kernel(a_ref, wg_ref, wu_ref, hist_ref, cw_ref, cb_ref, h_ref, gate_ref):
    a = a_ref[...]
    gate = _dot(a, wg_ref[...])
    up = _dot(a, wu_ref[...])
    gc = cb_ref[...] + hist_ref[0] * cw_ref[0:1, :] + hist_ref[1] * cw_ref[1:2, :] + gate * cw_ref[2:3, :]
    h_ref[...] = (jax.nn.gelu(gc) * up).astype(h_ref.dtype)
    gate_ref[...] = gate


def _ffn_in_sample(m2, w_ffn_in, hist_t, cw, cb):
    s, d = m2.shape
    dff = w_ffn_in.shape[1] // 2
    tn = 512
    nj = dff // tn
    return pl.pallas_call(
        _ffn_in_row_kernel,
        out_shape=[_sds((s, dff), BF16), _sds((s, dff), F32)],
        grid=(nj,),
        in_specs=[pl.BlockSpec((s, d), lambda j: (0, 0)),
                  pl.BlockSpec((d, tn), lambda j: (0, j)),
                  pl.BlockSpec((d, tn), lambda j: (0, nj + j)),
                  pl.BlockSpec((2, s, tn), lambda j: (0, 0, j)),
                  pl.BlockSpec((3, tn), lambda j: (0, j)),
                  pl.BlockSpec((1, tn), lambda j: (0, j))],
        out_specs=[pl.BlockSpec((s, tn), lambda j: (0, j)), pl.BlockSpec((s, tn), lambda j: (0, j))],
        compiler_params=_cp(("arbitrary",)),
        name="ffn_in_sample",
    )(m2, w_ffn_in, w_ffn_in, hist_t, cw, cb)


def _sscore_kernel(pt_ref, q_ref, w_ref, knew_ref, cache_ref, o_ref, kbuf, sem, *, n_pages, scale):
    s = pl.program_id(0)
    ns = pl.num_programs(0)
    page = kbuf.shape[3]

    def copies(seq, slot):
        return [pltpu.make_async_copy(cache_ref.at[pt_ref[seq, pg]], kbuf.at[slot, pg], sem.at[slot])
                for pg in range(n_pages)]

    @pl.when(s == 0)
    def _():
        for cp in copies(0, 0):
            cp.start()

    @pl.when(s + 1 < ns)
    def _():
        for cp in copies(s + 1, (s + 1) % 2):
            cp.start()

    slot = s % 2
    for cp in copies(s, slot):
        cp.wait()

    q = q_ref[...]
    qh = q.astype(BF16).astype(F32)
    q3 = jnp.concatenate([qh, q - qh, qh], axis=1).astype(BF16)
    w = w_ref[...] * scale
    for pg in range(n_pages):
        kt = kbuf[slot, pg]
        kh, kl = _split_bf16(kt)
        x = _dot(q3, jnp.concatenate([kh, kh, kl], axis=0))
        o_ref[:, pg * page:(pg + 1) * page] = jnp.sum(jnp.maximum(x, 0.0) * w, axis=0, keepdims=True)
    xs = jnp.sum(q * knew_ref[...], axis=1, keepdims=True)
    s_self = jnp.sum(jnp.maximum(xs, 0.0) * w, axis=0, keepdims=True)
    lane = lax.broadcasted_iota(I32, (1, LANES), 1)
    o_ref[:, n_pages * page:] = jnp.where(lane == 0, s_self, NEG_INF)


def _sample_scores(page_table, qi3, wi3, knew3, cache_kidx_t, scale):
    s, n_pages = page_table.shape
    _, idim, page = cache_kidx_t.shape
    h = qi3.shape[1]
    width = n_pages * page + LANES
    grid_spec = pltpu.PrefetchScalarGridSpec(
        num_scalar_prefetch=1,
        grid=(s,),
        in_specs=[pl.BlockSpec((None, h, idim), lambda i, pt: (i, 0, 0)),
                  pl.BlockSpec((None, h, 1), lambda i, pt: (i, 0, 0)),
                  pl.BlockSpec((None, 1, idim), lambda i, pt: (i, 0, 0)),
                  pl.BlockSpec(memory_space=pl.ANY)],
        out_specs=pl.BlockSpec((None, 1, width), lambda i, pt: (i, 0, 0)),
        scratch_shapes=[pltpu.VMEM((2, n_pages, idim, page), F32), pltpu.SemaphoreType.DMA((2,))],
    )
    return pl.pallas_call(
        functools.partial(_sscore_kernel, n_pages=n_pages, scale=scale),
        out_shape=_sds((s, 1, width), F32),
        grid_spec=grid_spec,
        compiler_params=_cp(("arbitrary",)),
        name="sample_scores",
    )(page_table, qi3, wi3, knew3, cache_kidx_t)


def _sselect_kernel(sc_ref, idx_ref, rank_s, *, kk, n_valid):
    s, width = sc_ref.shape
    nblk = width // LANES
    pos = lax.broadcasted_iota(I32, (s, width), 1)
    keys = jnp.where(pos < n_valid, _sortable(sc_ref[...] + 0.0), INT_MIN)
    cnt = lambda pred: jnp.sum(pred.astype(I32), axis=1, keepdims=True)
    thr = _kth_largest(lambda t: cnt(keys >= t), kk, s)
    gt = keys > thr
    eq = keys == thr
    need = (kk - cnt(gt)).astype(F32)
    tri = (lax.broadcasted_iota(I32, (LANES, LANES), 0) < lax.broadcasted_iota(I32, (LANES, LANES), 1)).astype(BF16)
    seen_eq = jnp.zeros((s, 1), F32)
    seen_sel = jnp.zeros((s, 1), F32)
    for b in range(nblk):
        sl = slice(b * LANES, (b + 1) * LANES)
        eqf = jnp.where(eq[:, sl], 1.0, 0.0)
        before_eq = seen_eq + _dot(eqf.astype(BF16), tri)
        sel = gt[:, sl] | (eq[:, sl] & (before_eq < need))
        self_f = jnp.where(sel, 1.0, 0.0)
        rank = seen_sel + _dot(self_f.astype(BF16), tri)
        rank_s[:, sl] = jnp.where(sel, rank, -1.0)
        seen_eq = seen_eq + jnp.sum(eqf, axis=1, keepdims=True)
        seen_sel = seen_sel + jnp.sum(self_f, axis=1, keepdims=True)
    jrow = lax.broadcasted_iota(I32, (width, LANES), 0)
    lcol = lax.broadcasted_iota(I32, (width, LANES), 1)
    parts = jnp.where(lcol == 0, jrow >> 7, jnp.where(lcol == 1, jrow & (LANES - 1), 0)).astype(F32).astype(BF16)
    r_iota = lax.broadcasted_iota(I32, (kk, width), 0).astype(F32)

    def body(q, carry):
        onehot = jnp.where(rank_s[pl.ds(q, 1), :] == r_iota, 1.0, 0.0).astype(BF16)
        res = _dot(onehot, parts)
        idx_ref[q] = (res[:, 0:1] * LANES + res[:, 1:2]).astype(I32)
        return carry
    lax.fori_loop(0, s, body, 0)


def _sample_select(scores, kk, n_valid):
    s, width = scores.shape
    return pl.pallas_call(
        functools.partial(_sselect_kernel, kk=kk, n_valid=n_valid),
        out_shape=_sds((s, kk, 1), I32),
        grid=(1,),
        in_specs=[pl.BlockSpec((s, width), lambda i: (0, 0))],
        out_specs=pl.BlockSpec((s, kk, 1), lambda i: (0, 0, 0)),
        scratch_shapes=[pltpu.VMEM((s, width), F32)],
        compiler_params=_cp(("arbitrary",)),
        name="sample_select",
    )(scores)


def _sattn_kernel(idx_ref, pt_ref, q_ref, idxc_ref, idxr_ref, rbh_ref, rbl_ref, knew_ref, vnew_ref, ck_ref, cv_ref, o_ref,
                  kbuf, vbuf, sem, *, kk, past_len, page, n_buckets):
    s = pl.program_id(0)
    ns = pl.num_programs(0)
    n_heads, dh = q_ref.shape
    page_bits = page.bit_length() - 1

    def issue(seq, slot):
        def body(r, carry):
            idx = jnp.minimum(idx_ref[seq, r], past_len - 1)
            phys = pt_ref[seq, lax.shift_right_logical(idx, page_bits)]
            off = idx & (page - 1)
            pltpu.make_async_copy(ck_ref.at[phys, off], kbuf.at[slot, :, r], sem.at[0, slot]).start()
            pltpu.make_async_copy(cv_ref.at[phys, off], vbuf.at[slot, :, r], sem.at[1, slot]).start()
            return carry
        lax.fori_loop(0, kk, body, 0, unroll=8)

    @pl.when(s == 0)
    def _():
        issue(0, 0)

    @pl.when(s + 1 < ns)
    def _():
        issue(s + 1, (s + 1) % 2)

    slot = s % 2
    pltpu.make_async_copy(kbuf.at[slot], kbuf.at[slot], sem.at[0, slot]).wait()
    pltpu.make_async_copy(vbuf.at[slot], vbuf.at[slot], sem.at[1, slot]).wait()

    idxr = idxr_ref[...]
    dist = past_len - idxr
    bkt = _bucket_jnp(dist, n_buckets)
    onehot = jnp.where(bkt == lax.broadcasted_iota(I32, (n_buckets, kk), 0), 1.0, 0.0).astype(BF16)
    bias = _dot(rbh_ref[...], onehot) + _dot(rbl_ref[...], onehot)
    is_new = idxc_ref[...] == past_len
    qb = q_ref[...].astype(BF16)
    rows = []
    for h in range(n_heads):
        kh = jnp.where(is_new, knew_ref[h:h + 1, :], kbuf[slot, h])
        rows.append(_dot_nt(qb, kh.astype(BF16))[h:h + 1, :])
    logit = jnp.concatenate(rows, axis=0) * (dh ** -0.5) + bias
    logit = jnp.where(dist >= 0, logit, NEG_INF)
    p = jnp.exp(logit - jnp.max(logit, axis=1, keepdims=True))
    pb = (p / jnp.sum(p, axis=1, keepdims=True)).astype(BF16)
    for h in range(n_heads):
        vh = jnp.where(is_new, vnew_ref[h:h + 1, :], vbuf[slot, h])
        o_ref[h:h + 1, :] = _dot(pb, vh.astype(BF16))[h:h + 1, :]


def _sample_attention(idx3, page_table, q3, rel_bias, knew3, vnew3, cache_k, cache_v, past_len):
    s, kk, _ = idx3.shape
    n_heads, dh = q3.shape[1:]
    page = cache_k.shape[1]
    assert page & (page - 1) == 0
    n_buckets = rel_bias.shape[0]
    rbt = rel_bias.T
    rbh = rbt.astype(BF16)
    rbl = (rbt - rbh.astype(F32)).astype(BF16)
    seq3 = lambda: pl.BlockSpec((None, n_heads, dh), lambda i, a, b: (i, 0, 0))
    grid_spec = pltpu.PrefetchScalarGridSpec(
        num_scalar_prefetch=2,
        grid=(s,),
        in_specs=[seq3(),
                  pl.BlockSpec((None, kk, 1), lambda i, a, b: (i, 0, 0)),
                  pl.BlockSpec((None, 1, kk), lambda i, a, b: (i, 0, 0)),
                  pl.BlockSpec((n_heads, n_buckets), lambda i, a, b: (0, 0)),
                  pl.BlockSpec((n_heads, n_buckets), lambda i, a, b: (0, 0)),
                  seq3(), seq3(),
                  pl.BlockSpec(memory_space=pl.ANY), pl.BlockSpec(memory_space=pl.ANY)],
        out_specs=seq3(),
        scratch_shapes=[pltpu.VMEM((2, n_heads, kk, dh), F32), pltpu.VMEM((2, n_heads, kk, dh), F32),
                        pltpu.SemaphoreType.DMA((2, 2))],
    )
    return pl.pallas_call(
        functools.partial(_sattn_kernel, kk=kk, past_len=past_len, page=page, n_buckets=n_buckets),
        out_shape=_sds((s, n_heads, dh), F32),
        grid_spec=grid_spec,
        compiler_params=_cp(("arbitrary",)),
        name="sample_attention",
    )(idx3.reshape(s, kk), page_table, q3, idx3, idx3.reshape(s, 1, kk), rbh, rbl, knew3, vnew3, cache_k, cache_v)


def kernel(x_prompt, x_sample, cache_k, cache_v, cache_kidx, page_table, state_lru_h, state_lru_conv, state_ffn_conv,
           c_prompt, c_sample, w_ada, b_ada, w_in, lru_conv_w, lru_conv_b, lru_w_a, lru_b_a, lru_w_x, lru_b_x,
           lru_lambda, attn_rel_bias, lru_out_g, attn_out_g, w_out, ln1_g, ln1_b, w_ffn_in, ffn_conv_w, ffn_conv_b,
           w_ffn_out, ln2_g, ln2_b):
    B, T, D = x_prompt.shape
    S, ts, _ = x_sample.shape
    assert ts == 1
    _, page, n_heads, dh = cache_k.shape
    idim = cache_kidx.shape[-1]
    n_pages = page_table.shape[1]
    past_len = n_pages * page
    W = lru_conv_b.shape[0]
    aw = n_heads * dh
    n_idx_heads = (w_in.shape[1] - 2 * W - 3 * aw - idim) // (idim + 1)
    nqi = n_idx_heads * idim
    dff = ffn_conv_b.shape[0]
    assert W % LANES == 0 and dh == LANES and 2 * idim == LANES and S % 8 == 0

    w_in_t = w_in.T
    w_in_b = w_in_t.astype(BF16)
    c0 = 2 * W + 3 * aw
    w_idx = jnp.concatenate([w_in_t[c0:c0 + nqi], w_in_t[c0 + nqi + n_idx_heads:],
                             w_in_t[c0 + nqi:c0 + nqi + n_idx_heads],
                             jnp.zeros((LANES - idim - n_idx_heads, D), F32)], axis=0)
    w_idx_hi = w_idx.astype(BF16)
    w_idx_lo = (w_idx - w_idx_hi.astype(F32)).astype(BF16)
    w_out_b = w_ffn_in_b = w_ffn_out_b = None
    lru_p = dict(cw=lru_conv_w, cb=lru_conv_b.reshape(1, W),
                 wax=jnp.concatenate([lru_w_a, lru_w_x], axis=2).astype(BF16),
                 ba=lru_b_a.reshape(1, W), bx=lru_b_x.reshape(1, W), lam=lru_lambda.reshape(1, W),
                 g=lru_out_g.reshape(1, W))
    ln1 = (ln1_g.reshape(1, D), ln1_b.reshape(1, D))
    ln2 = (ln2_g.reshape(1, D), ln2_b.reshape(1, D))
    fcw, fcb = ffn_conv_w, ffn_conv_b.reshape(1, dff)
    attn_g = attn_out_g.reshape(1, aw)

    mp = -(-(S + B) // 8) * 8
    c_all = jnp.concatenate([c_sample, c_prompt, jnp.zeros((mp - S - B, D), F32)], axis=0)
    mod = _ada(c_all, w_ada, b_ada)

    st_p = _Stream(B, T, min(1024, T), False, S)
    st_s = _Stream(1, S, S, True, 0)
    outs = {}
    for name, st, x3 in (("p", st_p, x_prompt), ("s", st_s, x_sample.reshape(1, S, D))):
        M = st.M
        m1 = _modulate(st, x3, mod, 0, 1).reshape(M, D)
        xg, = _matmul_nt(m1, w_in_b, 0, 2 * W, [F32], st.tm)
        q_f, q_b = _matmul_nt(m1, w_in_b, 2 * W, aw, [F32, BF16], st.tm)
        k_f, k_b = _matmul_nt(m1, w_in_b, 2 * W + aw, aw, [F32, BF16], st.tm)
        v_f, v_b = _matmul_nt(m1, w_in_b, 2 * W + 2 * aw, aw, [F32, BF16], st.tm)
        qi, wk = _idx_project(st, x3, mod, w_idx_hi, w_idx_lo, nqi)
        qi = qi.reshape(M, nqi)
        wk = wk.reshape(M, LANES)
        kidx = wk[:, :idim]
        if name == "p":
            kk = min(TOPK_MAX, T // 4)
            y_lru, h_last, w_out_b = _lru_prompt(xg, B, T, jnp.zeros((B, 8, W), F32), jnp.zeros((B, 1, W), F32), lru_p,
                                                 w_out)
            conv_state = xg.reshape(B, T, 2 * W)[:, T - 3:, :W]
            mask = _prompt_index(qi, wk, B, T, n_idx_heads, idim, kk)
            y_att, w_ffn_in_b = _prompt_attention(q_b, k_b, v_b, mask, attn_rel_bias, B, T, n_heads, dh, w_ffn_in)
        else:
            kk = min(TOPK_MAX, (past_len + 1) // 4)
            y_lru, h_last = _lru_sample(xg, jnp.swapaxes(state_lru_conv, 0, 1), state_lru_h, lru_p)
            conv_state = jnp.concatenate([state_lru_conv[:, 1:], xg[:, None, :W]], axis=1)
            scale = (n_idx_heads ** -0.5) * (idim ** -0.5)
            scores = _sample_scores(page_table, qi.reshape(S, n_idx_heads, idim),
                                    wk[:, idim:idim + n_idx_heads].reshape(S, n_idx_heads, 1),
                                    kidx.reshape(S, 1, idim), jnp.swapaxes(cache_kidx, 1, 2), scale)
            idx3 = _sample_select(scores.reshape(S, past_len + LANES), kk, past_len + 1)
            y_att = _sample_attention(idx3, page_table, q_f.reshape(S, n_heads, dh), attn_rel_bias,
                                      k_f.reshape(S, n_heads, dh), v_f.reshape(S, n_heads, dh),
                                      cache_k, cache_v, past_len).reshape(S, aw)
        y_att_n = _rms_norm(y_att, attn_g, 512)
        mix = _matmul_cat(y_lru, y_att_n, w_out_b, st.tm)
        x1, m2 = _residual_ln(st, x3, mix.reshape(x3.shape), mod, 2, *ln1, mod_next=(3, 4))
        m2 = m2.reshape(M, D)
        if name == "p":
            hmid, tail, w_ffn_out_b = _ffn_in_prompt(m2, w_ffn_in_b, jnp.zeros((B, 8, dff), F32), fcw, fcb, B, T,
                                                     w_ffn_out)
            ffn_state = tail[:, 6:, :]
        else:
            hmid, gate = _ffn_in_sample(m2, w_ffn_in_b, jnp.swapaxes(state_ffn_conv, 0, 1), fcw, fcb)
            ffn_state = jnp.concatenate([state_ffn_conv[:, 1:], gate[:, None, :]], axis=1)
        f = _matmul_ktiled(hmid, w_ffn_out_b, st.tm, 1024, 4096)
        y = _residual_ln(st, x1, f.reshape(x3.shape), mod, 5, *ln2)
        outs[name] = (y, k_f, v_f, kidx, h_last, conv_state, ffn_state)

    yp, kp, vp, kip, hp_, cp_, fp = outs["p"]
    ys, ks, vs, kis, hs_, cs_, fs = outs["s"]
    return (yp, ys.reshape(S, 1, D),
            kp.reshape(B, T, n_heads, dh), vp.reshape(B, T, n_heads, dh), kip.reshape(B, T, idim), hp_, cp_, fp,
            ks.reshape(S, 1, n_heads, dh), vs.reshape(S, 1, n_heads, dh), kis.reshape(S, 1, idim), hs_, cs_, fs)
```

```python
import functools
import math

import numpy as np
import jax
import jax.numpy as jnp
from jax import lax
from jax.experimental import pallas as pl
from jax.experimental.pallas import tpu as pltpu

F32 = jnp.float32
BF16 = jnp.bfloat16
I32 = jnp.int32

LRU_C = 8.0
TOPK_MAX = 256
MAX_DISTANCE = 128
LN_EPS = 1e-5
DEPTH = 1
DEEPNORM_ALPHA = (2.0 * DEPTH) ** 0.25
LANES = 128
VMEM_LIMIT = 56 * 1024 * 1024
INT_MIN = -(2 ** 31)
NEG_INF = float("-inf")


def _cp(sem, vmem=VMEM_LIMIT):
    return pltpu.CompilerParams(dimension_semantics=sem, vmem_limit_bytes=vmem)


def _sds(shape, dtype):
    return jax.ShapeDtypeStruct(shape, dtype)


def _split_bf16(x):
    hi = x.astype(BF16)
    lo = (x - hi.astype(F32)).astype(BF16)
    return hi, lo


def _dot(a, b):
    return jnp.dot(a, b, preferred_element_type=F32)


def _dot_nt(a, b):
    return lax.dot_general(a, b, (((1,), (1,)), ((), ())), preferred_element_type=F32)


def _dot3(a, b):
    ah, al = _split_bf16(a)
    bh, bl = _split_bf16(b)
    return _dot(ah, bh) + (_dot(al, bh) + _dot(ah, bl))


def _sortable(x):
    b = pltpu.bitcast(x, I32)
    return b ^ ((b >> 31) & 0x7FFFFFFF)


class _Stream:
    def __init__(self, G, R, tm, per_row, mod_row0):
        self.G, self.R, self.tm, self.per_row, self.mod_row0 = G, R, tm, per_row, mod_row0
        self.M = G * R
        self.nr = R // tm

    def mod_operand(self, mod2):
        if self.per_row:
            return mod2
        mp, n6 = mod2.shape
        return mod2.reshape(mp, 6, 1, n6 // 6)

    def mod_spec(self, which, width, d_model, col=lambda *ids: 0):
        if self.per_row:
            nb = d_model // width
            return pl.BlockSpec((self.tm, width), lambda g, i, *r: (i, which * nb + col(g, i, *r)))
        r0 = self.mod_row0
        return pl.BlockSpec((None, None, 1, width), lambda g, i, *r: (r0 + g, which, 0, col(g, i, *r)))


def _ada_kernel(c_ref, w_ref, b_ref, o_ref):
    c = c_ref[...]
    a = c * jax.nn.sigmoid(c)
    o_ref[...] = _dot3(a, w_ref[...]) + b_ref[...]


def _ada(c_all, w_ada, b_ada):
    mp, d = c_all.shape
    n = w_ada.shape[1]
    tn = 512
    return pl.pallas_call(
        _ada_kernel,
        out_shape=_sds((mp, n), F32),
        grid=(n // tn,),
        in_specs=[pl.BlockSpec((mp, d), lambda j: (0, 0)),
                  pl.BlockSpec((d, tn), lambda j: (0, j)),
                  pl.BlockSpec((1, tn), lambda j: (0, j))],
        out_specs=pl.BlockSpec((mp, tn), lambda j: (0, j)),
        compiler_params=_cp(("arbitrary",)),
        name="ada_mod",
    )(c_all, w_ada, b_ada.reshape(1, n))


def _mod_kernel(x_ref, sh_ref, sc_ref, o_ref):
    o_ref[...] = (x_ref[...] * (1.0 + sc_ref[...]) + sh_ref[...]).astype(o_ref.dtype)


def _modulate(st, x3, mod2, sh_which, sc_which):
    d = x3.shape[-1]
    tm = min(st.tm, 512)
    st2 = _Stream(st.G, st.R, tm, st.per_row, st.mod_row0)
    modop = st2.mod_operand(mod2)
    return pl.pallas_call(
        _mod_kernel,
        out_shape=_sds((st.G, st.R, d), BF16),
        grid=(st.G, st.R // tm),
        in_specs=[pl.BlockSpec((None, tm, d), lambda g, i: (g, i, 0)),
                  st2.mod_spec(sh_which, d, d), st2.mod_spec(sc_which, d, d)],
        out_specs=pl.BlockSpec((None, tm, d), lambda g, i: (g, i, 0)),
        compiler_params=_cp(("arbitrary", "arbitrary")),
        name="modulate",
    )(x3, modop, modop)


def _side_col(nj):
    return lambda i, j, *r: (0, jnp.where(i == 0, j, nj - 1))


def _mm_nt_kernel(a_ref, bt_ref, s_ref, *o_refs, n_main):
    r = _dot_nt(a_ref[...], bt_ref[...])
    for o in o_refs[:n_main]:
        o[...] = r.astype(o.dtype)

    @pl.when(pl.program_id(0) == 0)
    def _():
        r2 = _dot_nt(s_ref[...], bt_ref[...])
        for o in o_refs[n_main:]:
            o[...] = r2.astype(o.dtype)


def _matmul_nt(a, side, bt, col0, n, out_dtypes, side_dtypes, tm, tn=1024):
    m, k = a.shape
    ms = side.shape[0]
    tm = min(tm, m)
    tn = min(tn, n)
    assert m % tm == 0 and n % tn == 0 and col0 % tn == 0
    cb = col0 // tn
    nj = n // tn
    outs = pl.pallas_call(
        functools.partial(_mm_nt_kernel, n_main=len(out_dtypes)),
        out_shape=[_sds((m, n), dt) for dt in out_dtypes] + [_sds((ms, n), dt) for dt in side_dtypes],
        grid=(m // tm, nj),
        in_specs=[pl.BlockSpec((tm, k), lambda i, j: (i, 0)),
                  pl.BlockSpec((tn, k), lambda i, j: (cb + j, 0)),
                  pl.BlockSpec((ms, k), lambda i, j: (0, 0))],
        out_specs=[pl.BlockSpec((tm, tn), lambda i, j: (i, j)) for _ in out_dtypes]
        + [pl.BlockSpec((ms, tn), _side_col(nj)) for _ in side_dtypes],
        compiler_params=_cp(("arbitrary", "arbitrary")),
        name="matmul_nt",
    )(a, bt, side)
    return outs


def _mm2_kernel(a1_ref, a2_ref, b_ref, s1_ref, s2_ref, o_ref, os_ref):
    k1 = a1_ref.shape[1]
    o_ref[...] = _dot(a1_ref[...], b_ref[0:k1, :]) + _dot(a2_ref[...], b_ref[k1:, :])

    @pl.when(pl.program_id(0) == 0)
    def _():
        os_ref[...] = _dot(s1_ref[...], b_ref[0:k1, :]) + _dot(s2_ref[...], b_ref[k1:, :])


def _matmul_cat(a1, a2, s1, s2, b, tm, tn=1024):
    m, k1 = a1.shape
    k2 = a2.shape[1]
    ms = s1.shape[0]
    n = b.shape[1]
    tm = min(tm, m)
    nj = n // tn
    return pl.pallas_call(
        _mm2_kernel,
        out_shape=[_sds((m, n), F32), _sds((ms, n), F32)],
        grid=(m // tm, nj),
        in_specs=[pl.BlockSpec((tm, k1), lambda i, j: (i, 0)),
                  pl.BlockSpec((tm, k2), lambda i, j: (i, 0)),
                  pl.BlockSpec((k1 + k2, tn), lambda i, j: (0, j)),
                  pl.BlockSpec((ms, k1), lambda i, j: (0, 0)),
                  pl.BlockSpec((ms, k2), lambda i, j: (0, 0))],
        out_specs=[pl.BlockSpec((tm, tn), lambda i, j: (i, j)), pl.BlockSpec((ms, tn), _side_col(nj))],
        compiler_params=_cp(("arbitrary", "arbitrary")),
        name="matmul_cat",
    )(a1, a2, b, s1, s2)


def _mmk_kernel(a_ref, b_ref, s_ref, o_ref, os_ref):
    first_k = pl.program_id(2) == 0
    part = _dot(a_ref[...], b_ref[...])

    @pl.when(first_k)
    def _():
        o_ref[...] = part

    @pl.when(jnp.logical_not(first_k))
    def _():
        o_ref[...] += part

    @pl.when(pl.program_id(0) == 0)
    def _():
        part_s = _dot(s_ref[...], b_ref[...])

        @pl.when(first_k)
        def _():
            os_ref[...] = part_s

        @pl.when(jnp.logical_not(first_k))
        def _():
            os_ref[...] += part_s


def _matmul_ktiled(a, side, b, tm, tn, tk):
    m, k = a.shape
    ms = side.shape[0]
    n = b.shape[1]
    tm = min(tm, m)
    nj = n // tn
    return pl.pallas_call(
        _mmk_kernel,
        out_shape=[_sds((m, n), F32), _sds((ms, n), F32)],
        grid=(m // tm, nj, k // tk),
        in_specs=[pl.BlockSpec((tm, tk), lambda i, j, kk: (i, kk)),
                  pl.BlockSpec((tk, tn), lambda i, j, kk: (kk, j)),
                  pl.BlockSpec((ms, tk), lambda i, j, kk: (0, kk))],
        out_specs=[pl.BlockSpec((tm, tn), lambda i, j, kk: (i, j)), pl.BlockSpec((ms, tn), _side_col(nj))],
        compiler_params=_cp(("arbitrary", "arbitrary", "arbitrary")),
        name="matmul_ktiled",
    )(a, b, side)


def _idxproj_kernel(x_ref, sh_ref, sc_ref, wh_ref, wl_ref, q_ref, wk_ref, acc_ref, *, nq):
    kk = pl.program_id(2)
    m = x_ref[...] * (1.0 + sc_ref[...]) + sh_ref[...]
    mh, ml = _split_bf16(m)
    wh = wh_ref[...]
    part = _dot_nt(mh, wh) + (_dot_nt(ml, wh) + _dot_nt(mh, wl_ref[...]))

    @pl.when(kk == 0)
    def _():
        acc_ref[...] = part

    @pl.when(kk > 0)
    def _():
        acc_ref[...] += part

    @pl.when(kk == pl.num_programs(2) - 1)
    def _():
        q_ref[...] = acc_ref[:, 0:nq]
        wk_ref[...] = acc_ref[:, nq:]


def _idx_project(st, x3, mod2, w_hi, w_lo, nq):
    d = x3.shape[-1]
    nw = w_hi.shape[0]
    tm = min(st.tm, 512)
    tk = 1024
    st2 = _Stream(st.G, st.R, tm, st.per_row, st.mod_row0)
    modop = st2.mod_operand(mod2)
    kcol = lambda g, i, kk: kk
    q, wk = pl.pallas_call(
        functools.partial(_idxproj_kernel, nq=nq),
        out_shape=[_sds((st.G, st.R, nq), F32), _sds((st.G, st.R, nw - nq), F32)],
        grid=(st.G, st.R // tm, d // tk),
        in_specs=[pl.BlockSpec((None, tm, tk), lambda g, i, kk: (g, i, kk)),
                  st2.mod_spec(0, tk, d, kcol), st2.mod_spec(1, tk, d, kcol),
                  pl.BlockSpec((nw, tk), lambda g, i, kk: (0, kk)),
                  pl.BlockSpec((nw, tk), lambda g, i, kk: (0, kk))],
        out_specs=[pl.BlockSpec((None, tm, nq), lambda g, i, kk: (g, i, 0)),
                   pl.BlockSpec((None, tm, nw - nq), lambda g, i, kk: (g, i, 0))],
        scratch_shapes=[pltpu.VMEM((tm, nw), F32)],
        compiler_params=_cp(("arbitrary", "arbitrary", "arbitrary")),
        name="idx_project",
    )(x3, modop, modop, w_hi, w_lo)
    return q, wk


def _softplus(z):
    return jnp.maximum(z, 0.0) + jnp.log1p(jnp.exp(-jnp.abs(z)))


def _neg_expm1(x):
    poly = x * (-1.0 + x * (-1.0 / 2 + x * (-1.0 / 6 + x * (-1.0 / 24))))
    return jnp.where(x > -1.0 / 16, poly, 1.0 - jnp.exp(x))


def _lru_gates(xc, wax, ba, bx, lam):
    ri = _dot(xc.astype(BF16), wax)
    r = jax.nn.sigmoid(ri[:, :LANES] + ba)
    ig = jax.nn.sigmoid(ri[:, LANES:] + bx)
    log_a = (-LRU_C * r) * _softplus(-lam)
    a = jnp.exp(log_a)
    u = jnp.sqrt(_neg_expm1(2.0 * log_a)) * (ig * xc)
    return a, u


def _lru_prompt_kernel(xr_ref, gr_ref, hist_ref, h0_ref, cw_ref, cb_ref, wax_ref, ba_ref, bx_ref, lam_ref, g_ref,
                       wsrc_ref, y_ref, hlast_ref, wdst_ref, xbuf, hcar, ybuf):
    wdst_ref[...] = wsrc_ref[...].astype(wdst_ref.dtype)
    t = pl.program_id(1)
    tc, w = xr_ref.shape
    nb = w // LANES

    @pl.when(t == 0)
    def _():
        xbuf[0:8, :] = hist_ref[...]
        hcar[...] = h0_ref[...]

    @pl.when(t > 0)
    def _():
        xbuf[0:8, :] = xbuf[tc:tc + 8, :]

    xbuf[8:8 + tc, :] = xr_ref[...]
    rows3 = lax.broadcasted_iota(I32, (tc // 8, 8, LANES), 1)
    ssq = jnp.zeros((tc, 1), F32)
    for n in range(nb):
        sl = slice(n * LANES, (n + 1) * LANES)
        z3 = xbuf[0:8 + tc, sl].reshape(tc // 8 + 1, 8, LANES)
        xc = cb_ref[:, sl]
        for d in (3, 2, 1):
            rot = pltpu.roll(z3, d, axis=1)
            xc = xc + jnp.where(rows3 >= d, rot[1:], rot[:-1]).reshape(tc, LANES) * cw_ref[3 - d:4 - d, sl]
        xc = xc + z3[1:].reshape(tc, LANES) * cw_ref[3:4, sl]
        a, u = _lru_gates(xc, wax_ref[n], ba_ref[:, sl], bx_ref[:, sl], lam_ref[:, sl])
        a3 = a.reshape(tc // 8, 8, LANES)
        u3 = u.reshape(tc // 8, 8, LANES)
        for s in (1, 2, 4):
            keep = rows3 >= s
            u3 = jnp.where(keep, a3 * pltpu.roll(u3, s, axis=1) + u3, u3)
            a3 = jnp.where(keep, a3 * pltpu.roll(a3, s, axis=1), a3)
        h_in = hcar[:, sl]
        hs = []
        for g in range(tc // 8):
            hg = a3[g] * h_in + u3[g]
            hs.append(hg)
            h_in = hg[7:8, :]
        h = jnp.concatenate(hs, axis=0)
        hcar[:, sl] = h_in
        y = h * jax.nn.gelu(gr_ref[:, sl])
        ybuf[:, sl] = y
        ssq = ssq + jnp.sum(y * y, axis=1, keepdims=True)
    scale = lax.rsqrt(ssq * (1.0 / w) + LN_EPS)
    y_ref[...] = (ybuf[...] * scale * g_ref[...]).astype(y_ref.dtype)
    hlast_ref[...] = hcar[...]


def _lru_prompt(xg, B, T, hist8, h0, p, w_cast):
    w = xg.shape[1] // 2
    tc = min(256, T)
    nt = T // tc
    rc = _rider_rows(w_cast.shape[0], B * nt)
    wn = w_cast.shape[1]
    vec = lambda: pl.BlockSpec((1, w), lambda b, t: (0, 0))
    y, hl, w_b = pl.pallas_call(
        _lru_prompt_kernel,
        out_shape=[_sds((B * T, w), BF16), _sds((B, 1, w), F32), _sds(w_cast.shape, BF16)],
        grid=(B, nt),
        in_specs=[pl.BlockSpec((tc, w), lambda b, t: (b * nt + t, 0)),
                  pl.BlockSpec((tc, w), lambda b, t: (b * nt + t, 1)),
                  pl.BlockSpec((None, 8, w), lambda b, t: (b, 0, 0)),
                  pl.BlockSpec((None, 1, w), lambda b, t: (b, 0, 0)),
                  pl.BlockSpec((4, w), lambda b, t: (0, 0)), vec(),
                  pl.BlockSpec(p["wax"].shape, lambda b, t: (0, 0, 0)),
                  vec(), vec(), vec(), vec(),
                  pl.BlockSpec((rc, wn), lambda b, t: (b * nt + t, 0))],
        out_specs=[pl.BlockSpec((tc, w), lambda b, t: (b * nt + t, 0)),
                   pl.BlockSpec((None, 1, w), lambda b, t: (b, 0, 0)),
                   pl.BlockSpec((rc, wn), lambda b, t: (b * nt + t, 0))],
        scratch_shapes=[pltpu.VMEM((tc + 8, w), F32), pltpu.VMEM((1, w), F32), pltpu.VMEM((tc, w), F32)],
        compiler_params=_cp(("arbitrary", "arbitrary")),
        name="rglru_prompt",
    )(xg, xg, hist8, h0, p["cw"], p["cb"], p["wax"], p["ba"], p["bx"], p["lam"], p["g"], w_cast)
    return y, hl.reshape(B, w), w_b


def _lru_sample_kernel(xr_ref, gr_ref, hist_ref, h0_ref, cw_ref, cb_ref, wax_ref, ba_ref, bx_ref, lam_ref, g_ref,
                       y_ref, h_ref, ybuf):
    r, w = xr_ref.shape
    nb = w // LANES
    ssq = jnp.zeros((r, 1), F32)
    for n in range(nb):
        sl = slice(n * LANES, (n + 1) * LANES)
        xc = cb_ref[:, sl]
        for j in range(3):
            xc = xc + hist_ref[j, :, sl] * cw_ref[j:j + 1, sl]
        xc = xc + xr_ref[:, sl] * cw_ref[3:4, sl]
        a, u = _lru_gates(xc, wax_ref[n], ba_ref[:, sl], bx_ref[:, sl], lam_ref[:, sl])
        h = a * h0_ref[:, sl] + u
        h_ref[:, sl] = h
        y = h * jax.nn.gelu(gr_ref[:, sl])
        ybuf[:, sl] = y
        ssq = ssq + jnp.sum(y * y, axis=1, keepdims=True)
    scale = lax.rsqrt(ssq * (1.0 / w) + LN_EPS)
    y_ref[...] = (ybuf[...] * scale * g_ref[...]).astype(y_ref.dtype)


def _lru_sample(xg, hist_t, h0, p):
    s = xg.shape[0]
    w = xg.shape[1] // 2
    vec = lambda: pl.BlockSpec((1, w), lambda i: (0, 0))
    return pl.pallas_call(
        _lru_sample_kernel,
        out_shape=[_sds((s, w), BF16), _sds((s, w), F32)],
        grid=(1,),
        in_specs=[pl.BlockSpec((s, w), lambda i: (0, 0)), pl.BlockSpec((s, w), lambda i: (0, 1)),
                  pl.BlockSpec((3, s, w), lambda i: (0, 0, 0)), pl.BlockSpec((s, w), lambda i: (0, 0)),
                  pl.BlockSpec((4, w), lambda i: (0, 0)), vec(),
                  pl.BlockSpec(p["wax"].shape, lambda i: (0, 0, 0)),
                  vec(), vec(), vec(), vec()],
        out_specs=[pl.BlockSpec((s, w), lambda i: (0, 0)), pl.BlockSpec((s, w), lambda i: (0, 0))],
        scratch_shapes=[pltpu.VMEM((s, w), F32)],
        compiler_params=_cp(("arbitrary",)),
        name="rglru_sample",
    )(xg, xg, hist_t, h0, p["cw"], p["cb"], p["wax"], p["ba"], p["bx"], p["lam"], p["g"])


def _kth_largest(count_ge, kk, rows):
    def body(p, thr):
        bit = jnp.left_shift(jnp.int32(1), 31 - p)
        trial = thr + bit
        return jnp.where(count_ge(trial) >= kk, trial, thr)
    return lax.fori_loop(0, 32, body, jnp.full((rows, 1), INT_MIN, I32))


def _pidx_kernel(q_ref, wkq_ref, wkall_ref, o_ref, kcat, qcat, wb, keys, *, n_heads, idim, kk, kc):
    i = pl.program_id(1)
    tq = q_ref.shape[0]
    t_all = wkall_ref.shape[0]
    lane = lax.broadcasted_iota(I32, (tq, LANES), 1)

    @pl.when(i == 0)
    def _():
        k = wkall_ref[...]
        kh = k.astype(BF16).astype(F32)
        kl = k - kh
        left = lax.broadcasted_iota(I32, k.shape, 1) < idim
        kcat[:, 0:LANES] = jnp.where(left, kh, pltpu.roll(kl, idim, axis=1)).astype(BF16)
        kcat[:, LANES:] = jnp.where(left, kh, 0.0).astype(BF16)

    for pr in range(n_heads // 2):
        v = q_ref[:, pr * LANES:(pr + 1) * LANES]
        vh = v.astype(BF16).astype(F32)
        vl = v - vh
        vh_r = pltpu.roll(vh, idim, axis=1)
        vl_r = pltpu.roll(vl, idim, axis=1)
        first = lane < idim
        qcat[2 * pr, :, 0:LANES] = jnp.where(first, vh, vh_r).astype(BF16)
        qcat[2 * pr, :, LANES:] = jnp.where(first, vl, 0.0).astype(BF16)
        qcat[2 * pr + 1, :, 0:LANES] = jnp.where(first, vh_r, vh).astype(BF16)
        qcat[2 * pr + 1, :, LANES:] = jnp.where(first, vl_r, 0.0).astype(BF16)
    scale = (n_heads ** -0.5) * (idim ** -0.5)
    wq = wkq_ref[...] * scale
    for h in range(n_heads):
        wb[h] = jnp.broadcast_to(wq[:, idim + h:idim + h + 1], (tq, LANES))

    nch = (i * tq + tq + kc - 1) // kc
    qpos = i * tq + lax.broadcasted_iota(I32, (tq, kc), 0)
    cols = lax.broadcasted_iota(I32, (tq, kc), 1)

    def score_chunk(c, carry):
        k_c = kcat[pl.ds(pl.multiple_of(c * kc, kc), kc), :]
        x_all = _dot_nt(qcat[...].reshape(n_heads * tq, 2 * LANES), k_c)
        acc = jnp.zeros((tq, kc), F32)
        for h in range(n_heads):
            acc = acc + jnp.maximum(x_all[h * tq:(h + 1) * tq], 0.0) * jnp.tile(wb[h], (1, kc // LANES))
        key = jnp.where(c * kc + cols <= qpos, _sortable(acc + 0.0), INT_MIN)
        keys[:, pl.ds(pl.multiple_of(c * kc, kc), kc)] = key
        return carry

    lax.fori_loop(0, nch, score_chunk, 0)

    def count_where(preds):
        def body(c, cnts):
            kv = keys[:, pl.ds(pl.multiple_of(c * kc, kc), kc)]
            return [cnt + _fold_lanes(pred(kv).astype(I32), jnp.add) for cnt, pred in zip(cnts, preds)]
        cnts = lax.fori_loop(0, nch, body, [jnp.zeros((tq, LANES), I32) for _ in preds])
        return [jnp.sum(cnt, axis=1, keepdims=True) for cnt in cnts]

    thr = _kth_largest(lambda t: count_where([lambda kv: kv >= t])[0], kk, tq)
    n_ge, = count_where([lambda kv: (kv >= thr) & (kv > INT_MIN)])
    tie = jnp.max(n_ge) > kk

    o_ref[...] = jnp.full(o_ref.shape, NEG_INF, F32)

    @pl.when(jnp.logical_not(tie))
    def _():
        def body(c, carry):
            sl = pl.ds(pl.multiple_of(c * kc, kc), kc)
            kv = keys[:, sl]
            o_ref[:, sl] = jnp.where((kv >= thr) & (kv > INT_MIN), 0.0, NEG_INF)
            return carry
        lax.fori_loop(0, nch, body, 0)

    @pl.when(tie)
    def _():
        n_gt, = count_where([lambda kv: kv > thr])
        need = (kk - n_gt).astype(F32)
        tri = (lax.broadcasted_iota(I32, (kc, kc), 0) < lax.broadcasted_iota(I32, (kc, kc), 1)).astype(BF16)

        def body(c, seen):
            sl = pl.ds(pl.multiple_of(c * kc, kc), kc)
            kv = keys[:, sl]
            eq = (kv == thr) & (kv > INT_MIN)
            eqf = jnp.where(eq, 1.0, 0.0)
            before = seen + _dot(eqf.astype(BF16), tri)
            sel = (kv > thr) | (eq & (before < need))
            o_ref[:, sl] = jnp.where(sel, 0.0, NEG_INF)
            return seen + jnp.sum(eqf, axis=1, keepdims=True)
        lax.fori_loop(0, nch, body, jnp.zeros((tq, 1), F32))


def _prompt_index(qi, wk, B, T, n_heads, idim, kk):
    tq = min(256, T)
    kc = min(512, T)
    nq = T // tq
    return pl.pallas_call(
        functools.partial(_pidx_kernel, n_heads=n_heads, idim=idim, kk=kk, kc=kc),
        out_shape=_sds((B * T, T), F32),
        grid=(B, nq),
        in_specs=[pl.BlockSpec((tq, n_heads * idim), lambda b, i: (b * nq + i, 0)),
                  pl.BlockSpec((tq, LANES), lambda b, i: (b * nq + i, 0)),
                  pl.BlockSpec((T, LANES), lambda b, i: (b, 0))],
        out_specs=pl.BlockSpec((tq, T), lambda b, i: (b * nq + i, 0)),
        scratch_shapes=[pltpu.VMEM((T, 2 * LANES), BF16), pltpu.VMEM((n_heads, tq, 2 * LANES), BF16),
                        pltpu.VMEM((n_heads, tq, LANES), F32), pltpu.VMEM((tq, T), I32)],
        compiler_params=_cp(("arbitrary", "arbitrary")),
        name="prompt_index",
    )(qi, wk, wk)


def _bucket_np(d):
    n_buckets = 32
    max_exact = n_buckets // 2
    d = np.maximum(d, 0)
    large = max_exact + (np.log(np.maximum(d, 1).astype(np.float32) / np.float32(max_exact))
                         / np.float32(math.log(MAX_DISTANCE / max_exact))
                         * np.float32(n_buckets - max_exact)).astype(np.int32)
    large = np.minimum(large, n_buckets - 1)
    return np.where(d < max_exact, d, large).astype(np.int32)


def _bucket_jnp(d, n_buckets):
    max_exact = n_buckets // 2
    d = jnp.maximum(d, 0)
    large = max_exact + (jnp.log(jnp.maximum(d, 1).astype(F32) / max_exact)
                         / math.log(MAX_DISTANCE / max_exact) * (n_buckets - max_exact)).astype(I32)
    large = jnp.minimum(large, n_buckets - 1)
    return jnp.where(d < max_exact, d, large)


def _fold_lanes(x, op):
    out = x[:, 0:LANES]
    for s in range(1, x.shape[1] // LANES):
        out = op(out, x[:, s * LANES:(s + 1) * LANES])
    return out


def _pattn_kernel(rb_ref, q_ref, k_ref, v_ref, mask_ref, bkt_ref, wsrc_ref, o_ref, wdst_ref, tbl, sbuf, mx_s, l_s, acc_s,
                  *, hp, kc, n_far):
    wdst_ref[...] = wsrc_ref[...].astype(wdst_ref.dtype)
    hg = pl.program_id(1)
    i = pl.program_id(2)
    tq = q_ref.shape[0]
    dh = q_ref.shape[1] // hp
    log2e = math.log2(math.e)
    c_exp = dh ** -0.5 * log2e
    n_buckets = rb_ref.shape[0]

    @pl.when(i == 0)
    def _():
        tbl[...] = jnp.zeros(tbl.shape, F32)
        bkt = bkt_ref[:, kc - tq:kc + tq]
        for h in range(hp):
            far_b = rb_ref[n_far, hg * hp + h]
            acc = jnp.full(bkt.shape, far_b, F32)
            for b in range(n_buckets):
                acc = jnp.where(bkt == b, rb_ref[b, hg * hp + h], acc)
            tbl[h, :, kc - tq:kc + tq] = (acc - far_b) * log2e

    for h in range(hp):
        mx_s[h] = jnp.full((tq, LANES), NEG_INF, F32)
        l_s[h] = jnp.zeros((tq, LANES), F32)
        acc_s[h] = jnp.zeros((tq, dh), F32)

    nch = (i * tq + tq + kc - 1) // kc
    c_near = jnp.maximum((i * tq - tq) // kc, 0)

    def logits(c, bias_of):
        sl = pl.ds(pl.multiple_of(c * kc, kc), kc)
        msk = mask_ref[:, sl]
        for h in range(hp):
            hs = slice(h * dh, (h + 1) * dh)
            s = bias_of(h, _dot_nt(q_ref[:, hs], k_ref[sl, hs]) * c_exp + msk)
            sbuf[h, :, sl] = s
            mx_s[h] = jnp.maximum(mx_s[h], _fold_lanes(s, jnp.maximum))

    def far(c, carry):
        logits(c, lambda h, s: s)
        return carry

    def near(c, carry):
        off = pl.multiple_of(kc - (i * tq - c * kc), LANES)
        logits(c, lambda h, s: s + tbl[h, :, pl.ds(off, kc)])
        return carry

    lax.fori_loop(0, c_near, far, 0)
    lax.fori_loop(c_near, nch, near, 0)
    for h in range(hp):
        mx_s[h] = jnp.broadcast_to(jnp.max(mx_s[h], axis=1, keepdims=True), (tq, LANES))

    ones = jnp.ones((kc, LANES), BF16)

    def weigh(c, carry):
        sl = pl.ds(pl.multiple_of(c * kc, kc), kc)
        for h in range(hp):
            hs = slice(h * dh, (h + 1) * dh)
            p = jnp.exp2(sbuf[h, :, sl] - jnp.tile(mx_s[h], (1, kc // LANES))).astype(BF16)
            r = _dot(p, jnp.concatenate([v_ref[sl, hs], ones], axis=1))
            acc_s[h] = acc_s[h] + r[:, :dh]
            l_s[h] = l_s[h] + r[:, dh:]
        return carry

    lax.fori_loop(0, nch, weigh, 0)
    for h in range(hp):
        o_ref[:, h * dh:(h + 1) * dh] = acc_s[h] / l_s[h]


def _prompt_attention(q, k, v, mask, rel_bias, B, T, n_heads, dh, w_cast):
    tq = 128
    kc = min(1024, T)
    hp = 4
    nq = T // tq
    ng = n_heads // hp
    rc = _rider_rows(w_cast.shape[0], B * ng * nq)
    wn = w_cast.shape[1]
    n_buckets = rel_bias.shape[0]
    r = np.arange(tq)[:, None]
    x = np.arange(2 * kc)[None, :]
    dist = r + kc - x
    bkt = np.where(dist >= 0, _bucket_np(dist), -1).astype(np.int32)
    far_d = kc + tq
    assert _bucket_np(np.array([tq + 1]))[0] == n_buckets - 1 and far_d > tq
    grid_spec = pltpu.PrefetchScalarGridSpec(
        num_scalar_prefetch=0,
        grid=(B, n_heads // hp, nq),
        in_specs=[pl.BlockSpec(memory_space=pltpu.SMEM),
                  pl.BlockSpec((tq, hp * dh), lambda b, g, i: (b * nq + i, g)),
                  pl.BlockSpec((T, hp * dh), lambda b, g, i: (b, g)),
                  pl.BlockSpec((T, hp * dh), lambda b, g, i: (b, g)),
                  pl.BlockSpec((tq, T), lambda b, g, i: (b * nq + i, 0)),
                  pl.BlockSpec((tq, 2 * kc), lambda b, g, i: (0, 0)),
                  pl.BlockSpec((rc, wn), lambda b, g, i: ((b * ng + g) * nq + i, 0))],
        out_specs=[pl.BlockSpec((tq, hp * dh), lambda b, g, i: (b * nq + i, g)),
                   pl.BlockSpec((rc, wn), lambda b, g, i: ((b * ng + g) * nq + i, 0))],
        scratch_shapes=[pltpu.VMEM((hp, tq, 2 * kc), F32), pltpu.VMEM((hp, tq, T), F32),
                        pltpu.VMEM((hp, tq, LANES), F32), pltpu.VMEM((hp, tq, LANES), F32),
                        pltpu.VMEM((hp, tq, dh), F32)],
    )
    return pl.pallas_call(
        functools.partial(_pattn_kernel, hp=hp, kc=kc, n_far=n_buckets - 1),
        out_shape=[_sds((B * T, n_heads * dh), F32), _sds(w_cast.shape, BF16)],
        grid_spec=grid_spec,
        compiler_params=_cp(("arbitrary", "arbitrary", "arbitrary")),
        name="prompt_attention",
    )(rel_bias, q, k, v, mask, jnp.asarray(bkt), w_cast)


def _rms_kernel(x_ref, g_ref, o_ref):
    x = x_ref[...]
    ms = jnp.mean(x * x, axis=-1, keepdims=True)
    o_ref[...] = (x * lax.rsqrt(ms + LN_EPS) * g_ref[...]).astype(o_ref.dtype)


def _rms_norm(x, g, tm):
    m, w = x.shape
    tm = min(tm, m)
    return pl.pallas_call(
        _rms_kernel,
        out_shape=_sds((m, w), BF16),
        grid=(m // tm,),
        in_specs=[pl.BlockSpec((tm, w), lambda i: (i, 0)), pl.BlockSpec((1, w), lambda i: (0, 0))],
        out_specs=pl.BlockSpec((tm, w), lambda i: (i, 0)),
        compiler_params=_cp(("arbitrary",)),
        name="rms_norm",
    )(x, g)


def _layer_norm(z, g, b):
    mu = jnp.mean(z, axis=-1, keepdims=True)
    zc = z - mu
    var = jnp.mean(zc * zc, axis=-1, keepdims=True)
    return zc * lax.rsqrt(var + LN_EPS) * g + b


def _ln_mod_kernel(x_ref, f_ref, gate_ref, sh_ref, sc_ref, g_ref, b_ref, x1_ref, m_ref):
    z = DEEPNORM_ALPHA * x_ref[...] + gate_ref[...] * f_ref[...]
    x1 = _layer_norm(z, g_ref[...], b_ref[...])
    x1_ref[...] = x1
    m_ref[...] = (x1 * (1.0 + sc_ref[...]) + sh_ref[...]).astype(m_ref.dtype)


def _ln_kernel(x_ref, f_ref, gate_ref, g_ref, b_ref, y_ref):
    z = DEEPNORM_ALPHA * x_ref[...] + gate_ref[...] * f_ref[...]
    y_ref[...] = _layer_norm(z, g_ref[...], b_ref[...])


def _residual_ln(st, x3, f3, mod2, gate_which, ln_g, ln_b, mod_next=None):
    d = x3.shape[-1]
    tm = min(st.tm, 256)
    st2 = _Stream(st.G, st.R, tm, st.per_row, st.mod_row0)
    modop = st2.mod_operand(mod2)
    row = pl.BlockSpec((None, tm, d), lambda g, i: (g, i, 0))
    vec = pl.BlockSpec((1, d), lambda g, i: (0, 0))
    if mod_next is None:
        return pl.pallas_call(
            _ln_kernel,
            out_shape=_sds(x3.shape, F32),
            grid=(st.G, st.R // tm),
            in_specs=[row, row, st2.mod_spec(gate_which, d, d), vec, vec],
            out_specs=row,
            compiler_params=_cp(("arbitrary", "arbitrary")),
            name="residual_ln",
        )(x3, f3, modop, ln_g, ln_b)
    return pl.pallas_call(
        _ln_mod_kernel,
        out_shape=[_sds(x3.shape, F32), _sds(x3.shape, BF16)],
        grid=(st.G, st.R // tm),
        in_specs=[row, row, st2.mod_spec(gate_which, d, d), st2.mod_spec(mod_next[0], d, d),
                  st2.mod_spec(mod_next[1], d, d), vec, vec],
        out_specs=[row, row],
        compiler_params=_cp(("arbitrary", "arbitrary")),
        name="residual_ln_mod",
    )(x3, f3, modop, modop, modop, ln_g, ln_b)


def _rider_rows(n_rows, n_steps):
    rows = n_rows // n_steps
    assert rows * n_steps == n_rows and rows % 16 == 0, (n_rows, n_steps)
    return rows


def _ffn_in_seq_kernel(a_ref, wg_ref, wu_ref, hist_ref, cw_ref, cb_ref, wsrc_ref, s_ref, shist_ref,
                       h_ref, tail_ref, wdst_ref, hs_ref, gs_ref, carry, *, rows_per_seq):
    wdst_ref[...] = wsrc_ref[...].astype(wdst_ref.dtype)
    i = pl.program_id(1)

    @pl.when(i == 0)
    def _():
        gate = _dot(s_ref[...], wg_ref[...])
        up = _dot(s_ref[...], wu_ref[...])
        gc = cb_ref[...] + shist_ref[0] * cw_ref[0:1, :] + shist_ref[1] * cw_ref[1:2, :] + gate * cw_ref[2:3, :]
        hs_ref[...] = (jax.nn.gelu(gc) * up).astype(hs_ref.dtype)
        gs_ref[...] = gate

    tm = a_ref.shape[0]
    tiles_per_seq = rows_per_seq // tm
    tn = wg_ref.shape[1]
    cn = min(tn, 2 * LANES)

    @pl.when(i % tiles_per_seq == 0)
    def _():
        carry[...] = hist_ref[...]

    rows = lax.broadcasted_iota(I32, (tm, cn), 0)
    for c in range(tn // cn):
        cs = slice(c * cn, (c + 1) * cn)
        gate = _dot(a_ref[...], wg_ref[:, cs])
        up = _dot(a_ref[...], wu_ref[:, cs])
        prev = carry[:, cs]
        g1 = jnp.where(rows >= 1, pltpu.roll(gate, 1, axis=0), prev[7:8, :])
        g2 = jnp.where(rows >= 2, pltpu.roll(gate, 2, axis=0), jnp.where(rows == 1, prev[7:8, :], prev[6:7, :]))
        gc = cb_ref[:, cs] + g2 * cw_ref[0:1, cs] + g1 * cw_ref[1:2, cs] + gate * cw_ref[2:3, cs]
        h_ref[:, cs] = (jax.nn.gelu(gc) * up).astype(h_ref.dtype)
        carry[:, cs] = gate[tm - 8:tm, :]
        tail_ref[:, cs] = gate[tm - 8:tm, :]


def _ffn_in(m2, side, w_ffn_in, hist8, side_hist_t, cw, cb, B, T, w_cast):
    mt, d = m2.shape
    ms = side.shape[0]
    dff = w_ffn_in.shape[1] // 2
    tm = min(1024, T)
    tn = 512
    nj = dff // tn
    ni = mt // tm
    tps = T // tm
    rc = _rider_rows(w_cast.shape[0], nj * ni)
    wn = w_cast.shape[1]
    return pl.pallas_call(
        functools.partial(_ffn_in_seq_kernel, rows_per_seq=T),
        out_shape=[_sds((mt, dff), BF16), _sds((B, 8, dff), F32), _sds(w_cast.shape, BF16),
                   _sds((ms, dff), BF16), _sds((ms, dff), F32)],
        grid=(nj, ni),
        in_specs=[pl.BlockSpec((tm, d), lambda j, i: (i, 0)),
                  pl.BlockSpec((d, tn), lambda j, i: (0, j)),
                  pl.BlockSpec((d, tn), lambda j, i: (0, nj + j)),
                  pl.BlockSpec((None, 8, tn), lambda j, i: (i // tps, 0, j)),
                  pl.BlockSpec((3, tn), lambda j, i: (0, j)),
                  pl.BlockSpec((1, tn), lambda j, i: (0, j)),
                  pl.BlockSpec((rc, wn), lambda j, i: (j * ni + i, 0)),
                  pl.BlockSpec((ms, d), lambda j, i: (0, 0)),
                  pl.BlockSpec((2, ms, tn), lambda j, i: (0, 0, j))],
        out_specs=[pl.BlockSpec((tm, tn), lambda j, i: (i, j)),
                   pl.BlockSpec((None, 8, tn), lambda j, i: (i // tps, 0, j)),
                   pl.BlockSpec((rc, wn), lambda j, i: (j * ni + i, 0)),
                   pl.BlockSpec((ms, tn), lambda j, i: (0, j)),
                   pl.BlockSpec((ms, tn), lambda j, i: (0, j))],
        scratch_shapes=[pltpu.VMEM((8, tn), F32)],
        compiler_params=_cp(("arbitrary", "arbitrary")),
        name="ffn_in",
    )(m2, w_ffn_in, w_ffn_in, hist8, cw, cb, w_cast, side, side_hist_t)


def _sscore_kernel(pt_ref, q_ref, w_ref, knew_ref, cache_ref, o_ref, kbuf, sem, *, n_pages, scale):
    s = pl.program_id(0)
    ns = pl.num_programs(0)
    page = kbuf.shape[3]

    def copies(seq, slot):
        return [pltpu.make_async_copy(cache_ref.at[pt_ref[seq, pg]], kbuf.at[slot, pg], sem.at[slot])
                for pg in range(n_pages)]

    @pl.when(s == 0)
    def _():
        for cp in copies(0, 0):
            cp.start()

    @pl.when(s + 1 < ns)
    def _():
        for cp in copies(s + 1, (s + 1) % 2):
            cp.start()

    slot = s % 2
    for cp in copies(s, slot):
        cp.wait()

    q = q_ref[...]
    qh = q.astype(BF16).astype(F32)
    q3 = jnp.concatenate([qh, q - qh, qh], axis=1).astype(BF16)
    w = w_ref[...] * scale
    for pg in range(n_pages):
        kt = kbuf[slot, pg]
        kh, kl = _split_bf16(kt)
        x = _dot(q3, jnp.concatenate([kh, kh, kl], axis=0))
        o_ref[:, pg * page:(pg + 1) * page] = jnp.sum(jnp.maximum(x, 0.0) * w, axis=0, keepdims=True)
    xs = jnp.sum(q * knew_ref[...], axis=1, keepdims=True)
    s_self = jnp.sum(jnp.maximum(xs, 0.0) * w, axis=0, keepdims=True)
    lane = lax.broadcasted_iota(I32, (1, LANES), 1)
    o_ref[:, n_pages * page:] = jnp.where(lane == 0, s_self, NEG_INF)


def _sample_scores(page_table, qi3, wi3, knew3, cache_kidx_t, scale):
    s, n_pages = page_table.shape
    _, idim, page = cache_kidx_t.shape
    h = qi3.shape[1]
    width = n_pages * page + LANES
    grid_spec = pltpu.PrefetchScalarGridSpec(
        num_scalar_prefetch=1,
        grid=(s,),
        in_specs=[pl.BlockSpec((None, h, idim), lambda i, pt: (i, 0, 0)),
                  pl.BlockSpec((None, h, 1), lambda i, pt: (i, 0, 0)),
                  pl.BlockSpec((None, 1, idim), lambda i, pt: (i, 0, 0)),
                  pl.BlockSpec(memory_space=pl.ANY)],
        out_specs=pl.BlockSpec((None, 1, width), lambda i, pt: (i, 0, 0)),
        scratch_shapes=[pltpu.VMEM((2, n_pages, idim, page), F32), pltpu.SemaphoreType.DMA((2,))],
    )
    return pl.pallas_call(
        functools.partial(_sscore_kernel, n_pages=n_pages, scale=scale),
        out_shape=_sds((s, 1, width), F32),
        grid_spec=grid_spec,
        compiler_params=_cp(("arbitrary",)),
        name="sample_scores",
    )(page_table, qi3, wi3, knew3, cache_kidx_t)


def _sselect_kernel(sc_ref, idx_ref, rank_s, *, kk, n_valid):
    s, width = sc_ref.shape
    nblk = width // LANES
    pos = lax.broadcasted_iota(I32, (s, width), 1)
    keys = jnp.where(pos < n_valid, _sortable(sc_ref[...] + 0.0), INT_MIN)
    cnt = lambda pred: jnp.sum(pred.astype(I32), axis=1, keepdims=True)
    thr = _kth_largest(lambda t: cnt(keys >= t), kk, s)
    gt = keys > thr
    eq = keys == thr
    need = (kk - cnt(gt)).astype(F32)
    tri = (lax.broadcasted_iota(I32, (LANES, LANES), 0) < lax.broadcasted_iota(I32, (LANES, LANES), 1)).astype(BF16)
    seen_eq = jnp.zeros((s, 1), F32)
    seen_sel = jnp.zeros((s, 1), F32)
    for b in range(nblk):
        sl = slice(b * LANES, (b + 1) * LANES)
        eqf = jnp.where(eq[:, sl], 1.0, 0.0)
        before_eq = seen_eq + _dot(eqf.astype(BF16), tri)
        sel = gt[:, sl] | (eq[:, sl] & (before_eq < need))
        self_f = jnp.where(sel, 1.0, 0.0)
        rank = seen_sel + _dot(self_f.astype(BF16), tri)
        rank_s[:, sl] = jnp.where(sel, rank, -1.0)
        seen_eq = seen_eq + jnp.sum(eqf, axis=1, keepdims=True)
        seen_sel = seen_sel + jnp.sum(self_f, axis=1, keepdims=True)
    jrow = lax.broadcasted_iota(I32, (width, LANES), 0)
    lcol = lax.broadcasted_iota(I32, (width, LANES), 1)
    parts = jnp.where(lcol == 0, jrow >> 7, jnp.where(lcol == 1, jrow & (LANES - 1), 0)).astype(F32).astype(BF16)
    r_iota = lax.broadcasted_iota(I32, (kk, width), 0).astype(F32)

    def body(q, carry):
        onehot = jnp.where(rank_s[pl.ds(q, 1), :] == r_iota, 1.0, 0.0).astype(BF16)
        res = _dot(onehot, parts)
        idx_ref[q] = (res[:, 0:1] * LANES + res[:, 1:2]).astype(I32)
        return carry
    lax.fori_loop(0, s, body, 0)


def _sample_select(scores, kk, n_valid):
    s, width = scores.shape
    return pl.pallas_call(
        functools.partial(_sselect_kernel, kk=kk, n_valid=n_valid),
        out_shape=_sds((s, kk, 1), I32),
        grid=(1,),
        in_specs=[pl.BlockSpec((s, width), lambda i: (0, 0))],
        out_specs=pl.BlockSpec((s, kk, 1), lambda i: (0, 0, 0)),
        scratch_shapes=[pltpu.VMEM((s, width), F32)],
        compiler_params=_cp(("arbitrary",)),
        name="sample_select",
    )(scores)


def _sattn_kernel(idx_ref, pt_ref, q_ref, idxc_ref, idxr_ref, rbh_ref, rbl_ref, knew_ref, vnew_ref, ck_ref, cv_ref, o_ref,
                  kbuf, vbuf, sem, *, kk, past_len, page, n_buckets):
    s = pl.program_id(0)
    ns = pl.num_programs(0)
    n_heads, dh = q_ref.shape
    page_bits = page.bit_length() - 1

    def issue(seq, slot):
        def body(r, carry):
            idx = jnp.minimum(idx_ref[seq, r], past_len - 1)
            phys = pt_ref[seq, lax.shift_right_logical(idx, page_bits)]
            off = idx & (page - 1)
            pltpu.make_async_copy(ck_ref.at[phys, off], kbuf.at[slot, :, r], sem.at[0, slot]).start()
            pltpu.make_async_copy(cv_ref.at[phys, off], vbuf.at[slot, :, r], sem.at[1, slot]).start()
            return carry
        lax.fori_loop(0, kk, body, 0, unroll=8)

    @pl.when(s == 0)
    def _():
        issue(0, 0)

    @pl.when(s + 1 < ns)
    def _():
        issue(s + 1, (s + 1) % 2)

    slot = s % 2
    pltpu.make_async_copy(kbuf.at[slot], kbuf.at[slot], sem.at[0, slot]).wait()
    pltpu.make_async_copy(vbuf.at[slot], vbuf.at[slot], sem.at[1, slot]).wait()

    idxr = idxr_ref[...]
    dist = past_len - idxr
    bkt = _bucket_jnp(dist, n_buckets)
    onehot = jnp.where(bkt == lax.broadcasted_iota(I32, (n_buckets, kk), 0), 1.0, 0.0).astype(BF16)
    bias = _dot(rbh_ref[...], onehot) + _dot(rbl_ref[...], onehot)
    is_new = idxc_ref[...] == past_len
    qb = q_ref[...].astype(BF16)
    rows = []
    for h in range(n_heads):
        kh = jnp.where(is_new, knew_ref[h:h + 1, :], kbuf[slot, h])
        rows.append(_dot_nt(qb, kh.astype(BF16))[h:h + 1, :])
    logit = jnp.concatenate(rows, axis=0) * (dh ** -0.5) + bias
    logit = jnp.where(dist >= 0, logit, NEG_INF)
    p = jnp.exp(logit - jnp.max(logit, axis=1, keepdims=True))
    pb = (p / jnp.sum(p, axis=1, keepdims=True)).astype(BF16)
    for h in range(n_heads):
        vh = jnp.where(is_new, vnew_ref[h:h + 1, :], vbuf[slot, h])
        o_ref[h:h + 1, :] = _dot(pb, vh.astype(BF16))[h:h + 1, :]


def _sample_attention(idx3, page_table, q3, rel_bias, knew3, vnew3, cache_k, cache_v, past_len):
    s, kk, _ = idx3.shape
    n_heads, dh = q3.shape[1:]
    page = cache_k.shape[1]
    assert page & (page - 1) == 0
    n_buckets = rel_bias.shape[0]
    rbt = rel_bias.T
    rbh = rbt.astype(BF16)
    rbl = (rbt - rbh.astype(F32)).astype(BF16)
    seq3 = lambda: pl.BlockSpec((None, n_heads, dh), lambda i, a, b: (i, 0, 0))
    grid_spec = pltpu.PrefetchScalarGridSpec(
        num_scalar_prefetch=2,
        grid=(s,),
        in_specs=[seq3(),
                  pl.BlockSpec((None, kk, 1), lambda i, a, b: (i, 0, 0)),
                  pl.BlockSpec((None, 1, kk), lambda i, a, b: (i, 0, 0)),
                  pl.BlockSpec((n_heads, n_buckets), lambda i, a, b: (0, 0)),
                  pl.BlockSpec((n_heads, n_buckets), lambda i, a, b: (0, 0)),
                  seq3(), seq3(),
                  pl.BlockSpec(memory_space=pl.ANY), pl.BlockSpec(memory_space=pl.ANY)],
        out_specs=seq3(),
        scratch_shapes=[pltpu.VMEM((2, n_heads, kk, dh), F32), pltpu.VMEM((2, n_heads, kk, dh), F32),
                        pltpu.SemaphoreType.DMA((2, 2))],
    )
    return pl.pallas_call(
        functools.partial(_sattn_kernel, kk=kk, past_len=past_len, page=page, n_buckets=n_buckets),
        out_shape=_sds((s, n_heads, dh), F32),
        grid_spec=grid_spec,
        compiler_params=_cp(("arbitrary",)),
        name="sample_attention",
    )(idx3.reshape(s, kk), page_table, q3, idx3, idx3.reshape(s, 1, kk), rbh, rbl, knew3, vnew3, cache_k, cache_v)


def kernel(x_prompt, x_sample, cache_k, cache_v, cache_kidx, page_table, state_lru_h, state_lru_conv, state_ffn_conv,
           c_prompt, c_sample, w_ada, b_ada, w_in, lru_conv_w, lru_conv_b, lru_w_a, lru_b_a, lru_w_x, lru_b_x,
           lru_lambda, attn_rel_bias, lru_out_g, attn_out_g, w_out, ln1_g, ln1_b, w_ffn_in, ffn_conv_w, ffn_conv_b,
           w_ffn_out, ln2_g, ln2_b):
    B, T, D = x_prompt.shape
    S, ts, _ = x_sample.shape
    assert ts == 1
    _, page, n_heads, dh = cache_k.shape
    idim = cache_kidx.shape[-1]
    n_pages = page_table.shape[1]
    past_len = n_pages * page
    W = lru_conv_b.shape[0]
    aw = n_heads * dh
    n_idx_heads = (w_in.shape[1] - 2 * W - 3 * aw - idim) // (idim + 1)
    nqi = n_idx_heads * idim
    dff = ffn_conv_b.shape[0]
    assert W % LANES == 0 and dh == LANES and 2 * idim == LANES and S % 8 == 0

    w_in_t = w_in.T
    w_in_b = w_in_t.astype(BF16)
    c0 = 2 * W + 3 * aw
    w_idx = jnp.concatenate([w_in_t[c0:c0 + nqi], w_in_t[c0 + nqi + n_idx_heads:],
                             w_in_t[c0 + nqi:c0 + nqi + n_idx_heads],
                             jnp.zeros((LANES - idim - n_idx_heads, D), F32)], axis=0)
    w_idx_hi = w_idx.astype(BF16)
    w_idx_lo = (w_idx - w_idx_hi.astype(F32)).astype(BF16)
    lru_p = dict(cw=lru_conv_w, cb=lru_conv_b.reshape(1, W),
                 wax=jnp.concatenate([lru_w_a, lru_w_x], axis=2).astype(BF16),
                 ba=lru_b_a.reshape(1, W), bx=lru_b_x.reshape(1, W), lam=lru_lambda.reshape(1, W),
                 g=lru_out_g.reshape(1, W))
    ln1 = (ln1_g.reshape(1, D), ln1_b.reshape(1, D))
    ln2 = (ln2_g.reshape(1, D), ln2_b.reshape(1, D))
    fcw, fcb = ffn_conv_w, ffn_conv_b.reshape(1, dff)
    attn_g = attn_out_g.reshape(1, aw)

    mp = -(-(S + B) // 8) * 8
    c_all = jnp.concatenate([c_sample, c_prompt, jnp.zeros((mp - S - B, D), F32)], axis=0)
    mod = _ada(c_all, w_ada, b_ada)

    st_p = _Stream(B, T, min(1024, T), False, S)
    st_s = _Stream(1, S, S, True, 0)
    xs3 = x_sample.reshape(1, S, D)
    tm = st_p.tm
    MP = B * T

    m1_p = _modulate(st_p, x_prompt, mod, 0, 1).reshape(MP, D)
    m1_s = _modulate(st_s, xs3, mod, 0, 1).reshape(S, D)
    xg_p, xg_s = _matmul_nt(m1_p, m1_s, w_in_b, 0, 2 * W, [F32], [F32], tm)
    q_p, q_s = _matmul_nt(m1_p, m1_s, w_in_b, 2 * W, aw, [BF16], [F32], tm)
    k_p, kb_p, k_s = _matmul_nt(m1_p, m1_s, w_in_b, 2 * W + aw, aw, [F32, BF16], [F32], tm)
    v_p, vb_p, v_s = _matmul_nt(m1_p, m1_s, w_in_b, 2 * W + 2 * aw, aw, [F32, BF16], [F32], tm)
    qi_p, wk_p = _idx_project(st_p, x_prompt, mod, w_idx_hi, w_idx_lo, nqi)
    qi_s, wk_s = _idx_project(st_s, xs3, mod, w_idx_hi, w_idx_lo, nqi)
    qi_p, wk_p = qi_p.reshape(MP, nqi), wk_p.reshape(MP, LANES)
    qi_s, wk_s = qi_s.reshape(S, nqi), wk_s.reshape(S, LANES)
    kidx_p, kidx_s = wk_p[:, :idim], wk_s[:, :idim]

    ylru_p, h_p, w_out_b = _lru_prompt(xg_p, B, T, jnp.zeros((B, 8, W), F32), jnp.zeros((B, 1, W), F32), lru_p, w_out)
    conv_p = xg_p.reshape(B, T, 2 * W)[:, T - 3:, :W]
    mask = _prompt_index(qi_p, wk_p, B, T, n_idx_heads, idim, min(TOPK_MAX, T // 4))
    yatt_p, w_ffn_in_b = _prompt_attention(q_p, kb_p, vb_p, mask, attn_rel_bias, B, T, n_heads, dh, w_ffn_in)

    ylru_s, h_s = _lru_sample(xg_s, jnp.swapaxes(state_lru_conv, 0, 1), state_lru_h, lru_p)
    conv_s = jnp.concatenate([state_lru_conv[:, 1:], xg_s[:, None, :W]], axis=1)
    scores = _sample_scores(page_table, qi_s.reshape(S, n_idx_heads, idim),
                            wk_s[:, idim:idim + n_idx_heads].reshape(S, n_idx_heads, 1),
                            kidx_s.reshape(S, 1, idim), jnp.swapaxes(cache_kidx, 1, 2),
                            (n_idx_heads ** -0.5) * (idim ** -0.5))
    idx3 = _sample_select(scores.reshape(S, past_len + LANES), min(TOPK_MAX, (past_len + 1) // 4), past_len + 1)
    yatt_s = _sample_attention(idx3, page_table, q_s.reshape(S, n_heads, dh), attn_rel_bias,
                               k_s.reshape(S, n_heads, dh), v_s.reshape(S, n_heads, dh),
                               cache_k, cache_v, past_len).reshape(S, aw)

    mix_p, mix_s = _matmul_cat(ylru_p, _rms_norm(yatt_p, attn_g, 512), ylru_s, _rms_norm(yatt_s, attn_g, 512),
                               w_out_b, tm)
    x1_p, m2_p = _residual_ln(st_p, x_prompt, mix_p.reshape(B, T, D), mod, 2, *ln1, mod_next=(3, 4))
    x1_s, m2_s = _residual_ln(st_s, xs3, mix_s.reshape(1, S, D), mod, 2, *ln1, mod_next=(3, 4))
    hmid_p, tail, w_ffn_out_b, hmid_s, gate_s = _ffn_in(
        m2_p.reshape(MP, D), m2_s.reshape(S, D), w_ffn_in_b, jnp.zeros((B, 8, dff), F32),
        jnp.swapaxes(state_ffn_conv, 0, 1), fcw, fcb, B, T, w_ffn_out)
    ffn_p = tail[:, 6:, :]
    ffn_s = jnp.concatenate([state_ffn_conv[:, 1:], gate_s[:, None, :]], axis=1)
    f_p, f_s = _matmul_ktiled(hmid_p, hmid_s, w_ffn_out_b, tm, 1024, 4096)
    y_p = _residual_ln(st_p, x1_p, f_p.reshape(B, T, D), mod, 5, *ln2)
    y_s = _residual_ln(st_s, x1_s, f_s.reshape(1, S, D), mod, 5, *ln2)

    return (y_p, y_s.reshape(S, 1, D),
            k_p.reshape(B, T, n_heads, dh), v_p.reshape(B, T, n_heads, dh), kidx_p.reshape(B, T, idim), h_p, conv_p,
            ffn_p,
            k_s.reshape(S, 1, n_heads, dh), v_s.reshape(S, 1, n_heads, dh), kidx_s.reshape(S, 1, idim), h_s, conv_s,
            ffn_s)
```

```python
import functools
import math

import numpy as np
import jax
import jax.numpy as jnp
from jax import lax
from jax.experimental import pallas as pl
from jax.experimental.pallas import tpu as pltpu

F32 = jnp.float32
BF16 = jnp.bfloat16
I32 = jnp.int32

LRU_C = 8.0
TOPK_MAX = 256
MAX_DISTANCE = 128
LN_EPS = 1e-5
DEPTH = 1
DEEPNORM_ALPHA = (2.0 * DEPTH) ** 0.25
LANES = 128
VMEM_LIMIT = 56 * 1024 * 1024
INT_MIN = -(2 ** 31)
NEG_INF = float("-inf")


def _cp(sem, vmem=VMEM_LIMIT):
    return pltpu.CompilerParams(dimension_semantics=sem, vmem_limit_bytes=vmem)


def _sds(shape, dtype):
    return jax.ShapeDtypeStruct(shape, dtype)


def _split_bf16(x):
    hi = x.astype(BF16)
    lo = (x - hi.astype(F32)).astype(BF16)
    return hi, lo


def _dot(a, b):
    return jnp.dot(a, b, preferred_element_type=F32)


def _dot_nt(a, b):
    return lax.dot_general(a, b, (((1,), (1,)), ((), ())), preferred_element_type=F32)


def _dot3(a, b):
    ah, al = _split_bf16(a)
    bh, bl = _split_bf16(b)
    return _dot(ah, bh) + (_dot(al, bh) + _dot(ah, bl))


def _sortable(x):
    b = pltpu.bitcast(x, I32)
    return b ^ ((b >> 31) & 0x7FFFFFFF)


class _Stream:
    def __init__(self, G, R, tm, per_row, mod_row0):
        self.G, self.R, self.tm, self.per_row, self.mod_row0 = G, R, tm, per_row, mod_row0
        self.M = G * R
        self.nr = R // tm

    def mod_operand(self, mod2):
        if self.per_row:
            return mod2
        mp, n6 = mod2.shape
        return mod2.reshape(mp, 6, 1, n6 // 6)

    def mod_spec(self, which, width, d_model, col=lambda *ids: 0):
        if self.per_row:
            nb = d_model // width
            return pl.BlockSpec((self.tm, width), lambda g, i, *r: (i, which * nb + col(g, i, *r)))
        r0 = self.mod_row0
        return pl.BlockSpec((None, None, 1, width), lambda g, i, *r: (r0 + g, which, 0, col(g, i, *r)))


def _ada_kernel(c_ref, w_ref, b_ref, o_ref):
    c = c_ref[...]
    a = c * jax.nn.sigmoid(c)
    o_ref[...] = _dot3(a, w_ref[...]) + b_ref[...]


def _ada(c_all, w_ada, b_ada):
    mp, d = c_all.shape
    n = w_ada.shape[1]
    tn = 1024
    return pl.pallas_call(
        _ada_kernel,
        out_shape=_sds((mp, n), F32),
        grid=(n // tn,),
        in_specs=[pl.BlockSpec((mp, d), lambda j: (0, 0)),
                  pl.BlockSpec((d, tn), lambda j: (0, j)),
                  pl.BlockSpec((1, tn), lambda j: (0, j))],
        out_specs=pl.BlockSpec((mp, tn), lambda j: (0, j)),
        compiler_params=_cp(("arbitrary",)),
        name="ada_mod",
    )(c_all, w_ada, b_ada.reshape(1, n))


def _mod_kernel(x_ref, sh_ref, sc_ref, o_ref):
    o_ref[...] = (x_ref[...] * (1.0 + sc_ref[...]) + sh_ref[...]).astype(o_ref.dtype)


def _modulate(st, x3, mod2, sh_which, sc_which):
    d = x3.shape[-1]
    tm = min(st.tm, 512)
    st2 = _Stream(st.G, st.R, tm, st.per_row, st.mod_row0)
    modop = st2.mod_operand(mod2)
    return pl.pallas_call(
        _mod_kernel,
        out_shape=_sds((st.G, st.R, d), BF16),
        grid=(st.G, st.R // tm),
        in_specs=[pl.BlockSpec((None, tm, d), lambda g, i: (g, i, 0)),
                  st2.mod_spec(sh_which, d, d), st2.mod_spec(sc_which, d, d)],
        out_specs=pl.BlockSpec((None, tm, d), lambda g, i: (g, i, 0)),
        compiler_params=_cp(("arbitrary", "arbitrary")),
        name="modulate",
    )(x3, modop, modop)


def _side_col(nj):
    return lambda i, j, *r: (0, jnp.where(i == 0, j, nj - 1))


def _mm_nt_kernel(a_ref, bt_ref, s_ref, *o_refs, n_main):
    r = _dot_nt(a_ref[...], bt_ref[...])
    for o in o_refs[:n_main]:
        o[...] = r.astype(o.dtype)

    @pl.when(pl.program_id(0) == 0)
    def _():
        r2 = _dot_nt(s_ref[...], bt_ref[...])
        for o in o_refs[n_main:]:
            o[...] = r2.astype(o.dtype)


def _matmul_nt(a, side, bt, col0, n, out_dtypes, side_dtypes, tm, tn=1024):
    m, k = a.shape
    ms = side.shape[0]
    tm = min(tm, m)
    tn = min(tn, n)
    assert m % tm == 0 and n % tn == 0 and col0 % tn == 0
    cb = col0 // tn
    nj = n // tn
    outs = pl.pallas_call(
        functools.partial(_mm_nt_kernel, n_main=len(out_dtypes)),
        out_shape=[_sds((m, n), dt) for dt in out_dtypes] + [_sds((ms, n), dt) for dt in side_dtypes],
        grid=(m // tm, nj),
        in_specs=[pl.BlockSpec((tm, k), lambda i, j: (i, 0)),
                  pl.BlockSpec((tn, k), lambda i, j: (cb + j, 0)),
                  pl.BlockSpec((ms, k), lambda i, j: (0, 0))],
        out_specs=[pl.BlockSpec((tm, tn), lambda i, j: (i, j)) for _ in out_dtypes]
        + [pl.BlockSpec((ms, tn), _side_col(nj)) for _ in side_dtypes],
        compiler_params=_cp(("arbitrary", "arbitrary")),
        name="matmul_nt",
    )(a, bt, side)
    return outs


def _mm2_kernel(a1_ref, a2_ref, b_ref, s1_ref, s2_ref, o_ref, os_ref):
    k1 = a1_ref.shape[1]
    o_ref[...] = _dot(a1_ref[...], b_ref[0:k1, :]) + _dot(a2_ref[...], b_ref[k1:, :])

    @pl.when(pl.program_id(0) == 0)
    def _():
        os_ref[...] = _dot(s1_ref[...], b_ref[0:k1, :]) + _dot(s2_ref[...], b_ref[k1:, :])


def _matmul_cat(a1, a2, s1, s2, b, tm, tn=1024):
    m, k1 = a1.shape
    k2 = a2.shape[1]
    ms = s1.shape[0]
    n = b.shape[1]
    tm = min(tm, m)
    nj = n // tn
    return pl.pallas_call(
        _mm2_kernel,
        out_shape=[_sds((m, n), F32), _sds((ms, n), F32)],
        grid=(m // tm, nj),
        in_specs=[pl.BlockSpec((tm, k1), lambda i, j: (i, 0)),
                  pl.BlockSpec((tm, k2), lambda i, j: (i, 0)),
                  pl.BlockSpec((k1 + k2, tn), lambda i, j: (0, j)),
                  pl.BlockSpec((ms, k1), lambda i, j: (0, 0)),
                  pl.BlockSpec((ms, k2), lambda i, j: (0, 0))],
        out_specs=[pl.BlockSpec((tm, tn), lambda i, j: (i, j)), pl.BlockSpec((ms, tn), _side_col(nj))],
        compiler_params=_cp(("arbitrary", "arbitrary")),
        name="matmul_cat",
    )(a1, a2, b, s1, s2)


def _mmk_kernel(a_ref, b_ref, s_ref, o_ref, os_ref):
    first_k = pl.program_id(2) == 0
    part = _dot(a_ref[...], b_ref[...])

    @pl.when(first_k)
    def _():
        o_ref[...] = part

    @pl.when(jnp.logical_not(first_k))
    def _():
        o_ref[...] += part

    @pl.when(pl.program_id(0) == 0)
    def _():
        part_s = _dot(s_ref[...], b_ref[...])

        @pl.when(first_k)
        def _():
            os_ref[...] = part_s

        @pl.when(jnp.logical_not(first_k))
        def _():
            os_ref[...] += part_s


def _matmul_ktiled(a, side, b, tm, tn, tk):
    m, k = a.shape
    ms = side.shape[0]
    n = b.shape[1]
    tm = min(tm, m)
    nj = n // tn
    return pl.pallas_call(
        _mmk_kernel,
        out_shape=[_sds((m, n), F32), _sds((ms, n), F32)],
        grid=(m // tm, nj, k // tk),
        in_specs=[pl.BlockSpec((tm, tk), lambda i, j, kk: (i, kk)),
                  pl.BlockSpec((tk, tn), lambda i, j, kk: (kk, j)),
                  pl.BlockSpec((ms, tk), lambda i, j, kk: (0, kk))],
        out_specs=[pl.BlockSpec((tm, tn), lambda i, j, kk: (i, j)), pl.BlockSpec((ms, tn), _side_col(nj))],
        compiler_params=_cp(("arbitrary", "arbitrary", "arbitrary")),
        name="matmul_ktiled",
    )(a, b, side)


def _idxproj_kernel(x_ref, sh_ref, sc_ref, w_ref, *refs, nq, rider):
    if rider:
        wsrc_ref, q_ref, wk_ref, wdst_ref, acc_ref = refs
        wdst_ref[...] = wsrc_ref[...].astype(wdst_ref.dtype)
    else:
        q_ref, wk_ref, acc_ref = refs
    kk = pl.program_id(2)
    m = x_ref[...] * (1.0 + sc_ref[...]) + sh_ref[...]
    mh, ml = _split_bf16(m)
    wh, wl = _split_bf16(w_ref[...])
    part = _dot_nt(mh, wh) + (_dot_nt(ml, wh) + _dot_nt(mh, wl))

    @pl.when(kk == 0)
    def _():
        acc_ref[...] = part

    @pl.when(kk > 0)
    def _():
        acc_ref[...] += part

    @pl.when(kk == pl.num_programs(2) - 1)
    def _():
        q_ref[...] = acc_ref[:, 0:nq]
        wk_ref[...] = acc_ref[:, nq:]


def _idx_project(st, x3, mod2, w_idx, nq, w_cast=None, cast_rows=0):
    d = x3.shape[-1]
    nw = w_idx.shape[0]
    tm = min(st.tm, 512)
    tk = 1024
    ni, nk = st.R // tm, d // tk
    st2 = _Stream(st.G, st.R, tm, st.per_row, st.mod_row0)
    modop = st2.mod_operand(mod2)
    kcol = lambda g, i, kk: kk
    in_specs = [pl.BlockSpec((None, tm, tk), lambda g, i, kk: (g, i, kk)),
                st2.mod_spec(0, tk, d, kcol), st2.mod_spec(1, tk, d, kcol),
                pl.BlockSpec((nw, tk), lambda g, i, kk: (0, kk))]
    out_shape = [_sds((st.G, st.R, nq), F32), _sds((st.G, st.R, nw - nq), F32)]
    out_specs = [pl.BlockSpec((None, tm, nq), lambda g, i, kk: (g, i, 0)),
                 pl.BlockSpec((None, tm, nw - nq), lambda g, i, kk: (g, i, 0))]
    operands = [x3, modop, modop, w_idx]
    if w_cast is not None:
        rc = _rider_rows(cast_rows, st.G * ni * nk)
        slab = pl.BlockSpec((rc, w_cast.shape[1]), lambda g, i, kk: ((g * ni + i) * nk + kk, 0))
        in_specs.append(slab)
        out_shape.append(_sds((cast_rows, w_cast.shape[1]), BF16))
        out_specs.append(slab)
        operands.append(w_cast)
    return pl.pallas_call(
        functools.partial(_idxproj_kernel, nq=nq, rider=w_cast is not None),
        out_shape=out_shape,
        grid=(st.G, ni, nk),
        in_specs=in_specs,
        out_specs=out_specs,
        scratch_shapes=[pltpu.VMEM((tm, nw), F32)],
        compiler_params=_cp(("arbitrary", "arbitrary", "arbitrary")),
        name="idx_project",
    )(*operands)


def _softplus(z):
    return jnp.maximum(z, 0.0) + jnp.log1p(jnp.exp(-jnp.abs(z)))


def _neg_expm1(x):
    poly = x * (-1.0 + x * (-1.0 / 2 + x * (-1.0 / 6 + x * (-1.0 / 24))))
    return jnp.where(x > -1.0 / 16, poly, 1.0 - jnp.exp(x))


def _lru_gates(xc, wax, ba, bx, lam):
    ri = _dot(xc.astype(BF16), wax)
    r = jax.nn.sigmoid(ri[:, :LANES] + ba)
    ig = jax.nn.sigmoid(ri[:, LANES:] + bx)
    log_a = (-LRU_C * r) * _softplus(-lam)
    a = jnp.exp(log_a)
    u = jnp.sqrt(_neg_expm1(2.0 * log_a)) * (ig * xc)
    return a, u


def _lru_prompt_kernel(xr_ref, gr_ref, hist_ref, h0_ref, cw_ref, cb_ref, wax_ref, ba_ref, bx_ref, lam_ref, g_ref,
                       wsrc_ref, y_ref, hlast_ref, wdst_ref, xbuf, hcar, ybuf):
    wdst_ref[...] = wsrc_ref[...].astype(wdst_ref.dtype)
    t = pl.program_id(1)
    tc, w = xr_ref.shape
    nb = w // LANES

    @pl.when(t == 0)
    def _():
        xbuf[0:8, :] = hist_ref[...]
        hcar[...] = h0_ref[...]

    @pl.when(t > 0)
    def _():
        xbuf[0:8, :] = xbuf[tc:tc + 8, :]

    xbuf[8:8 + tc, :] = xr_ref[...]
    rows3 = lax.broadcasted_iota(I32, (tc // 8, 8, LANES), 1)
    ssq = jnp.zeros((tc, 1), F32)
    for n in range(nb):
        sl = slice(n * LANES, (n + 1) * LANES)
        z3 = xbuf[0:8 + tc, sl].reshape(tc // 8 + 1, 8, LANES)
        xc = cb_ref[:, sl]
        for d in (3, 2, 1):
            rot = pltpu.roll(z3, d, axis=1)
            xc = xc + jnp.where(rows3 >= d, rot[1:], rot[:-1]).reshape(tc, LANES) * cw_ref[3 - d:4 - d, sl]
        xc = xc + z3[1:].reshape(tc, LANES) * cw_ref[3:4, sl]
        a, u = _lru_gates(xc, wax_ref[n], ba_ref[:, sl], bx_ref[:, sl], lam_ref[:, sl])
        a3 = a.reshape(tc // 8, 8, LANES)
        u3 = u.reshape(tc // 8, 8, LANES)
        for s in (1, 2, 4):
            keep = rows3 >= s
            u3 = jnp.where(keep, a3 * pltpu.roll(u3, s, axis=1) + u3, u3)
            a3 = jnp.where(keep, a3 * pltpu.roll(a3, s, axis=1), a3)
        h_in = hcar[:, sl]
        hs = []
        for g in range(tc // 8):
            hg = a3[g] * h_in + u3[g]
            hs.append(hg)
            h_in = hg[7:8, :]
        h = jnp.concatenate(hs, axis=0)
        hcar[:, sl] = h_in
        y = h * jax.nn.gelu(gr_ref[:, sl])
        ybuf[:, sl] = y
        ssq = ssq + jnp.sum(y * y, axis=1, keepdims=True)
    scale = lax.rsqrt(ssq * (1.0 / w) + LN_EPS)
    y_ref[...] = (ybuf[...] * scale * g_ref[...]).astype(y_ref.dtype)
    hlast_ref[...] = hcar[...]


def _lru_prompt(xg, B, T, hist8, h0, p, w_cast):
    w = xg.shape[1] // 2
    tc = min(256, T)
    nt = T // tc
    rc = _rider_rows(w_cast.shape[0], B * nt)
    wn = w_cast.shape[1]
    vec = lambda: pl.BlockSpec((1, w), lambda b, t: (0, 0))
    y, hl, w_b = pl.pallas_call(
        _lru_prompt_kernel,
        out_shape=[_sds((B * T, w), BF16), _sds((B, 1, w), F32), _sds(w_cast.shape, BF16)],
        grid=(B, nt),
        in_specs=[pl.BlockSpec((tc, w), lambda b, t: (b * nt + t, 0)),
                  pl.BlockSpec((tc, w), lambda b, t: (b * nt + t, 1)),
                  pl.BlockSpec((None, 8, w), lambda b, t: (b, 0, 0)),
                  pl.BlockSpec((None, 1, w), lambda b, t: (b, 0, 0)),
                  pl.BlockSpec((4, w), lambda b, t: (0, 0)), vec(),
                  pl.BlockSpec(p["wax"].shape, lambda b, t: (0, 0, 0)),
                  vec(), vec(), vec(), vec(),
                  pl.BlockSpec((rc, wn), lambda b, t: (b * nt + t, 0))],
        out_specs=[pl.BlockSpec((tc, w), lambda b, t: (b * nt + t, 0)),
                   pl.BlockSpec((None, 1, w), lambda b, t: (b, 0, 0)),
                   pl.BlockSpec((rc, wn), lambda b, t: (b * nt + t, 0))],
        scratch_shapes=[pltpu.VMEM((tc + 8, w), F32), pltpu.VMEM((1, w), F32), pltpu.VMEM((tc, w), F32)],
        compiler_params=_cp(("arbitrary", "arbitrary")),
        name="rglru_prompt",
    )(xg, xg, hist8, h0, p["cw"], p["cb"], p["wax"], p["ba"], p["bx"], p["lam"], p["g"], w_cast)
    return y, hl.reshape(B, w), w_b


def _lru_sample_kernel(xr_ref, gr_ref, hist_ref, h0_ref, cw_ref, cb_ref, wax_ref, ba_ref, bx_ref, lam_ref, g_ref,
                       y_ref, h_ref, ybuf):
    r, w = xr_ref.shape
    nb = w // LANES
    ssq = jnp.zeros((r, 1), F32)
    for n in range(nb):
        sl = slice(n * LANES, (n + 1) * LANES)
        xc = cb_ref[:, sl]
        for j in range(3):
            xc = xc + hist_ref[j, :, sl] * cw_ref[j:j + 1, sl]
        xc = xc + xr_ref[:, sl] * cw_ref[3:4, sl]
        a, u = _lru_gates(xc, wax_ref[n], ba_ref[:, sl], bx_ref[:, sl], lam_ref[:, sl])
        h = a * h0_ref[:, sl] + u
        h_ref[:, sl] = h
        y = h * jax.nn.gelu(gr_ref[:, sl])
        ybuf[:, sl] = y
        ssq = ssq + jnp.sum(y * y, axis=1, keepdims=True)
    scale = lax.rsqrt(ssq * (1.0 / w) + LN_EPS)
    y_ref[...] = (ybuf[...] * scale * g_ref[...]).astype(y_ref.dtype)


def _lru_sample(xg, hist_t, h0, p):
    s = xg.shape[0]
    w = xg.shape[1] // 2
    vec = lambda: pl.BlockSpec((1, w), lambda i: (0, 0))
    return pl.pallas_call(
        _lru_sample_kernel,
        out_shape=[_sds((s, w), BF16), _sds((s, w), F32)],
        grid=(1,),
        in_specs=[pl.BlockSpec((s, w), lambda i: (0, 0)), pl.BlockSpec((s, w), lambda i: (0, 1)),
                  pl.BlockSpec((3, s, w), lambda i: (0, 0, 0)), pl.BlockSpec((s, w), lambda i: (0, 0)),
                  pl.BlockSpec((4, w), lambda i: (0, 0)), vec(),
                  pl.BlockSpec(p["wax"].shape, lambda i: (0, 0, 0)),
                  vec(), vec(), vec(), vec()],
        out_specs=[pl.BlockSpec((s, w), lambda i: (0, 0)), pl.BlockSpec((s, w), lambda i: (0, 0))],
        scratch_shapes=[pltpu.VMEM((s, w), F32)],
        compiler_params=_cp(("arbitrary",)),
        name="rglru_sample",
    )(xg, xg, hist_t, h0, p["cw"], p["cb"], p["wax"], p["ba"], p["bx"], p["lam"], p["g"])


def _kth_largest(count_ge, kk, rows):
    def body(p, thr):
        bit = jnp.left_shift(jnp.int32(1), 31 - p)
        trial = thr + bit
        return jnp.where(count_ge(trial) >= kk, trial, thr)
    return lax.fori_loop(0, 32, body, jnp.full((rows, 1), INT_MIN, I32))


def _pidx_kernel(q_ref, wkq_ref, wkall_ref, o_ref, kcat, qcat, wb, keys, *, n_heads, idim, kk, kc):
    i = pl.program_id(1)
    tq = q_ref.shape[0]
    t_all = wkall_ref.shape[0]
    lane = lax.broadcasted_iota(I32, (tq, LANES), 1)

    @pl.when(i == 0)
    def _():
        k = wkall_ref[...]
        kh = k.astype(BF16).astype(F32)
        kl = k - kh
        left = lax.broadcasted_iota(I32, k.shape, 1) < idim
        kcat[:, 0:LANES] = jnp.where(left, kh, pltpu.roll(kl, idim, axis=1)).astype(BF16)
        kcat[:, LANES:] = jnp.where(left, kh, 0.0).astype(BF16)

    for pr in range(n_heads // 2):
        v = q_ref[:, pr * LANES:(pr + 1) * LANES]
        vh = v.astype(BF16).astype(F32)
        vl = v - vh
        vh_r = pltpu.roll(vh, idim, axis=1)
        vl_r = pltpu.roll(vl, idim, axis=1)
        first = lane < idim
        qcat[2 * pr, :, 0:LANES] = jnp.where(first, vh, vh_r).astype(BF16)
        qcat[2 * pr, :, LANES:] = jnp.where(first, vl, 0.0).astype(BF16)
        qcat[2 * pr + 1, :, 0:LANES] = jnp.where(first, vh_r, vh).astype(BF16)
        qcat[2 * pr + 1, :, LANES:] = jnp.where(first, vl_r, 0.0).astype(BF16)
    scale = (n_heads ** -0.5) * (idim ** -0.5)
    wq = wkq_ref[...] * scale
    for h in range(n_heads):
        wb[h] = jnp.broadcast_to(wq[:, idim + h:idim + h + 1], (tq, LANES))

    nch = (i * tq + tq + kc - 1) // kc
    qpos = i * tq + lax.broadcasted_iota(I32, (tq, kc), 0)
    cols = lax.broadcasted_iota(I32, (tq, kc), 1)

    def score_chunk(c, carry):
        k_c = kcat[pl.ds(pl.multiple_of(c * kc, kc), kc), :]
        x_all = _dot_nt(qcat[...].reshape(n_heads * tq, 2 * LANES), k_c)
        acc = jnp.zeros((tq, kc), F32)
        for h in range(n_heads):
            acc = acc + jnp.maximum(x_all[h * tq:(h + 1) * tq], 0.0) * jnp.tile(wb[h], (1, kc // LANES))
        key = jnp.where(c * kc + cols <= qpos, _sortable(acc + 0.0), INT_MIN)
        keys[:, pl.ds(pl.multiple_of(c * kc, kc), kc)] = key
        return carry

    lax.fori_loop(0, nch, score_chunk, 0)

    def count_where(preds):
        def body(c, cnts):
            kv = keys[:, pl.ds(pl.multiple_of(c * kc, kc), kc)]
            return [cnt + _fold_lanes(pred(kv).astype(I32), jnp.add) for cnt, pred in zip(cnts, preds)]
        cnts = lax.fori_loop(0, nch, body, [jnp.zeros((tq, LANES), I32) for _ in preds])
        return [jnp.sum(cnt, axis=1, keepdims=True) for cnt in cnts]

    thr = _kth_largest(lambda t: count_where([lambda kv: kv >= t])[0], kk, tq)
    n_ge, = count_where([lambda kv: (kv >= thr) & (kv > INT_MIN)])
    tie = jnp.max(n_ge) > kk

    o_ref[...] = jnp.full(o_ref.shape, NEG_INF, F32)

    @pl.when(jnp.logical_not(tie))
    def _():
        def body(c, carry):
            sl = pl.ds(pl.multiple_of(c * kc, kc), kc)
            kv = keys[:, sl]
            o_ref[:, sl] = jnp.where((kv >= thr) & (kv > INT_MIN), 0.0, NEG_INF)
            return carry
        lax.fori_loop(0, nch, body, 0)

    @pl.when(tie)
    def _():
        n_gt, = count_where([lambda kv: kv > thr])
        need = (kk - n_gt).astype(F32)
        tri = (lax.broadcasted_iota(I32, (kc, kc), 0) < lax.broadcasted_iota(I32, (kc, kc), 1)).astype(BF16)

        def body(c, seen):
            sl = pl.ds(pl.multiple_of(c * kc, kc), kc)
            kv = keys[:, sl]
            eq = (kv == thr) & (kv > INT_MIN)
            eqf = jnp.where(eq, 1.0, 0.0)
            before = seen + _dot(eqf.astype(BF16), tri)
            sel = (kv > thr) | (eq & (before < need))
            o_ref[:, sl] = jnp.where(sel, 0.0, NEG_INF)
            return seen + jnp.sum(eqf, axis=1, keepdims=True)
        lax.fori_loop(0, nch, body, jnp.zeros((tq, 1), F32))


def _prompt_index(qi, wk, B, T, n_heads, idim, kk):
    tq = min(256, T)
    kc = min(512, T)
    nq = T // tq
    return pl.pallas_call(
        functools.partial(_pidx_kernel, n_heads=n_heads, idim=idim, kk=kk, kc=kc),
        out_shape=_sds((B * T, T), F32),
        grid=(B, nq),
        in_specs=[pl.BlockSpec((tq, n_heads * idim), lambda b, i: (b * nq + i, 0)),
                  pl.BlockSpec((tq, LANES), lambda b, i: (b * nq + i, 0)),
                  pl.BlockSpec((T, LANES), lambda b, i: (b, 0))],
        out_specs=pl.BlockSpec((tq, T), lambda b, i: (b * nq + i, 0)),
        scratch_shapes=[pltpu.VMEM((T, 2 * LANES), BF16), pltpu.VMEM((n_heads, tq, 2 * LANES), BF16),
                        pltpu.VMEM((n_heads, tq, LANES), F32), pltpu.VMEM((tq, T), I32)],
        compiler_params=_cp(("arbitrary", "arbitrary")),
        name="prompt_index",
    )(qi, wk, wk)


def _bucket_np(d):
    n_buckets = 32
    max_exact = n_buckets // 2
    d = np.maximum(d, 0)
    large = max_exact + (np.log(np.maximum(d, 1).astype(np.float32) / np.float32(max_exact))
                         / np.float32(math.log(MAX_DISTANCE / max_exact))
                         * np.float32(n_buckets - max_exact)).astype(np.int32)
    large = np.minimum(large, n_buckets - 1)
    return np.where(d < max_exact, d, large).astype(np.int32)


def _bucket_jnp(d, n_buckets):
    max_exact = n_buckets // 2
    d = jnp.maximum(d, 0)
    large = max_exact + (jnp.log(jnp.maximum(d, 1).astype(F32) / max_exact)
                         / math.log(MAX_DISTANCE / max_exact) * (n_buckets - max_exact)).astype(I32)
    large = jnp.minimum(large, n_buckets - 1)
    return jnp.where(d < max_exact, d, large)


def _fold_lanes(x, op):
    out = x[:, 0:LANES]
    for s in range(1, x.shape[1] // LANES):
        out = op(out, x[:, s * LANES:(s + 1) * LANES])
    return out


def _pattn_kernel(rb_ref, q_ref, k_ref, v_ref, mask_ref, bkt_ref, wsrc_ref, o_ref, wdst_ref, tbl, sbuf, mx_s, l_s, acc_s,
                  *, hp, kc, n_far):
    wdst_ref[...] = wsrc_ref[...].astype(wdst_ref.dtype)
    hg = pl.program_id(1)
    i = pl.program_id(2)
    tq = q_ref.shape[0]
    dh = q_ref.shape[1] // hp
    log2e = math.log2(math.e)
    c_exp = dh ** -0.5 * log2e
    n_buckets = rb_ref.shape[0]

    @pl.when(i == 0)
    def _():
        tbl[...] = jnp.zeros(tbl.shape, F32)
        bkt = bkt_ref[:, kc - tq:kc + tq]
        for h in range(hp):
            far_b = rb_ref[n_far, hg * hp + h]
            acc = jnp.full(bkt.shape, far_b, F32)
            for b in range(n_buckets):
                acc = jnp.where(bkt == b, rb_ref[b, hg * hp + h], acc)
            tbl[h, :, kc - tq:kc + tq] = (acc - far_b) * log2e

    for h in range(hp):
        mx_s[h] = jnp.full((tq, LANES), NEG_INF, F32)
        l_s[h] = jnp.zeros((tq, LANES), F32)
        acc_s[h] = jnp.zeros((tq, dh), F32)

    nch = (i * tq + tq + kc - 1) // kc
    c_near = jnp.maximum((i * tq - tq) // kc, 0)

    def logits(c, bias_of):
        sl = pl.ds(pl.multiple_of(c * kc, kc), kc)
        msk = mask_ref[:, sl]
        for h in range(hp):
            hs = slice(h * dh, (h + 1) * dh)
            s = bias_of(h, _dot_nt(q_ref[:, hs], k_ref[sl, hs]) * c_exp + msk)
            sbuf[h, :, sl] = s
            mx_s[h] = jnp.maximum(mx_s[h], _fold_lanes(s, jnp.maximum))

    def far(c, carry):
        logits(c, lambda h, s: s)
        return carry

    def near(c, carry):
        off = pl.multiple_of(kc - (i * tq - c * kc), LANES)
        logits(c, lambda h, s: s + tbl[h, :, pl.ds(off, kc)])
        return carry

    lax.fori_loop(0, c_near, far, 0)
    lax.fori_loop(c_near, nch, near, 0)
    for h in range(hp):
        mx_s[h] = jnp.broadcast_to(jnp.max(mx_s[h], axis=1, keepdims=True), (tq, LANES))

    ones = jnp.ones((kc, LANES), BF16)

    def weigh(c, carry):
        sl = pl.ds(pl.multiple_of(c * kc, kc), kc)
        for h in range(hp):
            hs = slice(h * dh, (h + 1) * dh)
            p = jnp.exp2(sbuf[h, :, sl] - jnp.tile(mx_s[h], (1, kc // LANES))).astype(BF16)
            r = _dot(p, jnp.concatenate([v_ref[sl, hs], ones], axis=1))
            acc_s[h] = acc_s[h] + r[:, :dh]
            l_s[h] = l_s[h] + r[:, dh:]
        return carry

    lax.fori_loop(0, nch, weigh, 0)
    for h in range(hp):
        o_ref[:, h * dh:(h + 1) * dh] = acc_s[h] / l_s[h]


def _prompt_attention(q, k, v, mask, rel_bias, B, T, n_heads, dh, w_cast):
    tq = 128
    kc = min(1024, T)
    hp = 4
    nq = T // tq
    ng = n_heads // hp
    rc = _rider_rows(w_cast.shape[0], B * ng * nq)
    wn = w_cast.shape[1]
    n_buckets = rel_bias.shape[0]
    r = np.arange(tq)[:, None]
    x = np.arange(2 * kc)[None, :]
    dist = r + kc - x
    bkt = np.where(dist >= 0, _bucket_np(dist), -1).astype(np.int32)
    far_d = kc + tq
    assert _bucket_np(np.array([tq + 1]))[0] == n_buckets - 1 and far_d > tq
    grid_spec = pltpu.PrefetchScalarGridSpec(
        num_scalar_prefetch=0,
        grid=(B, n_heads // hp, nq),
        in_specs=[pl.BlockSpec(memory_space=pltpu.SMEM),
                  pl.BlockSpec((tq, hp * dh), lambda b, g, i: (b * nq + i, g)),
                  pl.BlockSpec((T, hp * dh), lambda b, g, i: (b, g)),
                  pl.BlockSpec((T, hp * dh), lambda b, g, i: (b, g)),
                  pl.BlockSpec((tq, T), lambda b, g, i: (b * nq + i, 0)),
                  pl.BlockSpec((tq, 2 * kc), lambda b, g, i: (0, 0)),
                  pl.BlockSpec((rc, wn), lambda b, g, i: ((b * ng + g) * nq + i, 0))],
        out_specs=[pl.BlockSpec((tq, hp * dh), lambda b, g, i: (b * nq + i, g)),
                   pl.BlockSpec((rc, wn), lambda b, g, i: ((b * ng + g) * nq + i, 0))],
        scratch_shapes=[pltpu.VMEM((hp, tq, 2 * kc), F32), pltpu.VMEM((hp, tq, T), F32),
                        pltpu.VMEM((hp, tq, LANES), F32), pltpu.VMEM((hp, tq, LANES), F32),
                        pltpu.VMEM((hp, tq, dh), F32)],
    )
    return pl.pallas_call(
        functools.partial(_pattn_kernel, hp=hp, kc=kc, n_far=n_buckets - 1),
        out_shape=[_sds((B * T, n_heads * dh), F32), _sds(w_cast.shape, BF16)],
        grid_spec=grid_spec,
        compiler_params=_cp(("arbitrary", "arbitrary", "arbitrary")),
        name="prompt_attention",
    )(rel_bias, q, k, v, mask, jnp.asarray(bkt), w_cast)


def _rms_kernel(x_ref, g_ref, o_ref):
    x = x_ref[...]
    ms = jnp.mean(x * x, axis=-1, keepdims=True)
    o_ref[...] = (x * lax.rsqrt(ms + LN_EPS) * g_ref[...]).astype(o_ref.dtype)


def _rms_norm(x, g, tm):
    m, w = x.shape
    tm = min(tm, m)
    return pl.pallas_call(
        _rms_kernel,
        out_shape=_sds((m, w), BF16),
        grid=(m // tm,),
        in_specs=[pl.BlockSpec((tm, w), lambda i: (i, 0)), pl.BlockSpec((1, w), lambda i: (0, 0))],
        out_specs=pl.BlockSpec((tm, w), lambda i: (i, 0)),
        compiler_params=_cp(("arbitrary",)),
        name="rms_norm",
    )(x, g)


def _layer_norm(z, g, b):
    mu = jnp.mean(z, axis=-1, keepdims=True)
    zc = z - mu
    var = jnp.mean(zc * zc, axis=-1, keepdims=True)
    return zc * lax.rsqrt(var + LN_EPS) * g + b


def _ln_mod_kernel(x_ref, f_ref, gate_ref, sh_ref, sc_ref, g_ref, b_ref, x1_ref, m_ref):
    z = DEEPNORM_ALPHA * x_ref[...] + gate_ref[...] * f_ref[...]
    x1 = _layer_norm(z, g_ref[...], b_ref[...])
    x1_ref[...] = x1
    m_ref[...] = (x1 * (1.0 + sc_ref[...]) + sh_ref[...]).astype(m_ref.dtype)


def _ln_kernel(x_ref, f_ref, gate_ref, g_ref, b_ref, y_ref):
    z = DEEPNORM_ALPHA * x_ref[...] + gate_ref[...] * f_ref[...]
    y_ref[...] = _layer_norm(z, g_ref[...], b_ref[...])


def _residual_ln(st, x3, f3, mod2, gate_which, ln_g, ln_b, mod_next=None):
    d = x3.shape[-1]
    tm = min(st.tm, 256)
    st2 = _Stream(st.G, st.R, tm, st.per_row, st.mod_row0)
    modop = st2.mod_operand(mod2)
    row = pl.BlockSpec((None, tm, d), lambda g, i: (g, i, 0))
    vec = pl.BlockSpec((1, d), lambda g, i: (0, 0))
    if mod_next is None:
        return pl.pallas_call(
            _ln_kernel,
            out_shape=_sds(x3.shape, F32),
            grid=(st.G, st.R // tm),
            in_specs=[row, row, st2.mod_spec(gate_which, d, d), vec, vec],
            out_specs=row,
            compiler_params=_cp(("arbitrary", "arbitrary")),
            name="residual_ln",
        )(x3, f3, modop, ln_g, ln_b)
    return pl.pallas_call(
        _ln_mod_kernel,
        out_shape=[_sds(x3.shape, F32), _sds(x3.shape, BF16)],
        grid=(st.G, st.R // tm),
        in_specs=[row, row, st2.mod_spec(gate_which, d, d), st2.mod_spec(mod_next[0], d, d),
                  st2.mod_spec(mod_next[1], d, d), vec, vec],
        out_specs=[row, row],
        compiler_params=_cp(("arbitrary", "arbitrary")),
        name="residual_ln_mod",
    )(x3, f3, modop, modop, modop, ln_g, ln_b)


def _rider_rows(n_rows, n_steps):
    rows = n_rows // n_steps
    assert rows * n_steps == n_rows and rows % 16 == 0, (n_rows, n_steps)
    return rows


def _ffn_in_seq_kernel(a_ref, wg_ref, wu_ref, hist_ref, cw_ref, cb_ref, wsrc_ref, s_ref, shist_ref,
                       h_ref, tail_ref, wdst_ref, hs_ref, gs_ref, carry, *, rows_per_seq):
    wdst_ref[...] = wsrc_ref[...].astype(wdst_ref.dtype)
    i = pl.program_id(1)

    @pl.when(i == 0)
    def _():
        gate = _dot(s_ref[...], wg_ref[...])
        up = _dot(s_ref[...], wu_ref[...])
        gc = cb_ref[...] + shist_ref[0] * cw_ref[0:1, :] + shist_ref[1] * cw_ref[1:2, :] + gate * cw_ref[2:3, :]
        hs_ref[...] = (jax.nn.gelu(gc) * up).astype(hs_ref.dtype)
        gs_ref[...] = gate

    tm = a_ref.shape[0]
    tiles_per_seq = rows_per_seq // tm
    tn = wg_ref.shape[1]
    cn = min(tn, 2 * LANES)

    @pl.when(i % tiles_per_seq == 0)
    def _():
        carry[...] = hist_ref[...]

    rows = lax.broadcasted_iota(I32, (tm, cn), 0)
    for c in range(tn // cn):
        cs = slice(c * cn, (c + 1) * cn)
        gate = _dot(a_ref[...], wg_ref[:, cs])
        up = _dot(a_ref[...], wu_ref[:, cs])
        prev = carry[:, cs]
        g1 = jnp.where(rows >= 1, pltpu.roll(gate, 1, axis=0), prev[7:8, :])
        g2 = jnp.where(rows >= 2, pltpu.roll(gate, 2, axis=0), jnp.where(rows == 1, prev[7:8, :], prev[6:7, :]))
        gc = cb_ref[:, cs] + g2 * cw_ref[0:1, cs] + g1 * cw_ref[1:2, cs] + gate * cw_ref[2:3, cs]
        h_ref[:, cs] = (jax.nn.gelu(gc) * up).astype(h_ref.dtype)
        carry[:, cs] = gate[tm - 8:tm, :]
        tail_ref[:, cs] = gate[tm - 8:tm, :]


def _ffn_in(m2, side, w_ffn_in, hist8, side_hist_t, cw, cb, B, T, w_cast):
    mt, d = m2.shape
    ms = side.shape[0]
    dff = w_ffn_in.shape[1] // 2
    tm = min(1024, T)
    tn = 512
    nj = dff // tn
    ni = mt // tm
    tps = T // tm
    rc = _rider_rows(w_cast.shape[0], nj * ni)
    wn = w_cast.shape[1]
    return pl.pallas_call(
        functools.partial(_ffn_in_seq_kernel, rows_per_seq=T),
        out_shape=[_sds((mt, dff), BF16), _sds((B, 8, dff), F32), _sds(w_cast.shape, BF16),
                   _sds((ms, dff), BF16), _sds((ms, dff), F32)],
        grid=(nj, ni),
        in_specs=[pl.BlockSpec((tm, d), lambda j, i: (i, 0)),
                  pl.BlockSpec((d, tn), lambda j, i: (0, j)),
                  pl.BlockSpec((d, tn), lambda j, i: (0, nj + j)),
                  pl.BlockSpec((None, 8, tn), lambda j, i: (i // tps, 0, j)),
                  pl.BlockSpec((3, tn), lambda j, i: (0, j)),
                  pl.BlockSpec((1, tn), lambda j, i: (0, j)),
                  pl.BlockSpec((rc, wn), lambda j, i: (j * ni + i, 0)),
                  pl.BlockSpec((ms, d), lambda j, i: (0, 0)),
                  pl.BlockSpec((2, ms, tn), lambda j, i: (0, 0, j))],
        out_specs=[pl.BlockSpec((tm, tn), lambda j, i: (i, j)),
                   pl.BlockSpec((None, 8, tn), lambda j, i: (i // tps, 0, j)),
                   pl.BlockSpec((rc, wn), lambda j, i: (j * ni + i, 0)),
                   pl.BlockSpec((ms, tn), lambda j, i: (0, j)),
                   pl.BlockSpec((ms, tn), lambda j, i: (0, j))],
        scratch_shapes=[pltpu.VMEM((8, tn), F32)],
        compiler_params=_cp(("arbitrary", "arbitrary")),
        name="ffn_in",
    )(m2, w_ffn_in, w_ffn_in, hist8, cw, cb, w_cast, side, side_hist_t)


def _sscore_kernel(pt_ref, q_ref, w_ref, knew_ref, cache_ref, o_ref, kbuf, sem, *, n_pages, scale):
    s = pl.program_id(0)
    ns = pl.num_programs(0)
    page = kbuf.shape[3]

    def copies(seq, slot):
        return [pltpu.make_async_copy(cache_ref.at[pt_ref[seq, pg]], kbuf.at[slot, pg], sem.at[slot])
                for pg in range(n_pages)]

    @pl.when(s == 0)
    def _():
        for cp in copies(0, 0):
            cp.start()

    @pl.when(s + 1 < ns)
    def _():
        for cp in copies(s + 1, (s + 1) % 2):
            cp.start()

    slot = s % 2
    for cp in copies(s, slot):
        cp.wait()

    q = q_ref[...]
    qh = q.astype(BF16).astype(F32)
    q3 = jnp.concatenate([qh, q - qh, qh], axis=1).astype(BF16)
    w = w_ref[...] * scale
    for pg in range(n_pages):
        kt = kbuf[slot, pg]
        kh, kl = _split_bf16(kt)
        x = _dot(q3, jnp.concatenate([kh, kh, kl], axis=0))
        o_ref[:, pg * page:(pg + 1) * page] = jnp.sum(jnp.maximum(x, 0.0) * w, axis=0, keepdims=True)
    xs = jnp.sum(q * knew_ref[...], axis=1, keepdims=True)
    s_self = jnp.sum(jnp.maximum(xs, 0.0) * w, axis=0, keepdims=True)
    lane = lax.broadcasted_iota(I32, (1, LANES), 1)
    o_ref[:, n_pages * page:] = jnp.where(lane == 0, s_self, NEG_INF)


def _sample_scores(page_table, qi3, wi3, knew3, cache_kidx_t, scale):
    s, n_pages = page_table.shape
    _, idim, page = cache_kidx_t.shape
    h = qi3.shape[1]
    width = n_pages * page + LANES
    grid_spec = pltpu.PrefetchScalarGridSpec(
        num_scalar_prefetch=1,
        grid=(s,),
        in_specs=[pl.BlockSpec((None, h, idim), lambda i, pt: (i, 0, 0)),
                  pl.BlockSpec((None, h, 1), lambda i, pt: (i, 0, 0)),
                  pl.BlockSpec((None, 1, idim), lambda i, pt: (i, 0, 0)),
                  pl.BlockSpec(memory_space=pl.ANY)],
        out_specs=pl.BlockSpec((None, 1, width), lambda i, pt: (i, 0, 0)),
        scratch_shapes=[pltpu.VMEM((2, n_pages, idim, page), F32), pltpu.SemaphoreType.DMA((2,))],
    )
    return pl.pallas_call(
        functools.partial(_sscore_kernel, n_pages=n_pages, scale=scale),
        out_shape=_sds((s, 1, width), F32),
        grid_spec=grid_spec,
        compiler_params=_cp(("arbitrary",)),
        name="sample_scores",
    )(page_table, qi3, wi3, knew3, cache_kidx_t)


def _sselect_kernel(sc_ref, idx_ref, rank_s, *, kk, n_valid):
    s, width = sc_ref.shape
    nblk = width // LANES
    pos = lax.broadcasted_iota(I32, (s, width), 1)
    keys = jnp.where(pos < n_valid, _sortable(sc_ref[...] + 0.0), INT_MIN)
    cnt = lambda pred: jnp.sum(pred.astype(I32), axis=1, keepdims=True)
    thr = _kth_largest(lambda t: cnt(keys >= t), kk, s)
    gt = keys > thr
    eq = keys == thr
    need = (kk - cnt(gt)).astype(F32)
    tri = (lax.broadcasted_iota(I32, (LANES, LANES), 0) < lax.broadcasted_iota(I32, (LANES, LANES), 1)).astype(BF16)
    seen_eq = jnp.zeros((s, 1), F32)
    seen_sel = jnp.zeros((s, 1), F32)
    for b in range(nblk):
        sl = slice(b * LANES, (b + 1) * LANES)
        eqf = jnp.where(eq[:, sl], 1.0, 0.0)
        before_eq = seen_eq + _dot(eqf.astype(BF16), tri)
        sel = gt[:, sl] | (eq[:, sl] & (before_eq < need))
        self_f = jnp.where(sel, 1.0, 0.0)
        rank = seen_sel + _dot(self_f.astype(BF16), tri)
        rank_s[:, sl] = jnp.where(sel, rank, -1.0)
        seen_eq = seen_eq + jnp.sum(eqf, axis=1, keepdims=True)
        seen_sel = seen_sel + jnp.sum(self_f, axis=1, keepdims=True)
    jrow = lax.broadcasted_iota(I32, (width, LANES), 0)
    lcol = lax.broadcasted_iota(I32, (width, LANES), 1)
    parts = jnp.where(lcol == 0, jrow >> 7, jnp.where(lcol == 1, jrow & (LANES - 1), 0)).astype(F32).astype(BF16)
    r_iota = lax.broadcasted_iota(I32, (kk, width), 0).astype(F32)

    def body(q, carry):
        onehot = jnp.where(rank_s[pl.ds(q, 1), :] == r_iota, 1.0, 0.0).astype(BF16)
        res = _dot(onehot, parts)
        idx_ref[q] = (res[:, 0:1] * LANES + res[:, 1:2]).astype(I32)
        return carry
    lax.fori_loop(0, s, body, 0)


def _sample_select(scores, kk, n_valid):
    s, width = scores.shape
    return pl.pallas_call(
        functools.partial(_sselect_kernel, kk=kk, n_valid=n_valid),
        out_shape=_sds((s, kk, 1), I32),
        grid=(1,),
        in_specs=[pl.BlockSpec((s, width), lambda i: (0, 0))],
        out_specs=pl.BlockSpec((s, kk, 1), lambda i: (0, 0, 0)),
        scratch_shapes=[pltpu.VMEM((s, width), F32)],
        compiler_params=_cp(("arbitrary",)),
        name="sample_select",
    )(scores)


def _sattn_kernel(idx_ref, pt_ref, q_ref, idxc_ref, idxr_ref, rbh_ref, rbl_ref, knew_ref, vnew_ref, ck_ref, cv_ref, o_ref,
                  kbuf, vbuf, sem, *, kk, past_len, page, n_buckets):
    s = pl.program_id(0)
    ns = pl.num_programs(0)
    n_heads, dh = q_ref.shape
    page_bits = page.bit_length() - 1

    def issue(seq, slot):
        def body(r, carry):
            idx = jnp.minimum(idx_ref[seq, r], past_len - 1)
            phys = pt_ref[seq, lax.shift_right_logical(idx, page_bits)]
            off = idx & (page - 1)
            pltpu.make_async_copy(ck_ref.at[phys, off], kbuf.at[slot, :, r], sem.at[0, slot]).start()
            pltpu.make_async_copy(cv_ref.at[phys, off], vbuf.at[slot, :, r], sem.at[1, slot]).start()
            return carry
        lax.fori_loop(0, kk, body, 0, unroll=8)

    @pl.when(s == 0)
    def _():
        issue(0, 0)

    @pl.when(s + 1 < ns)
    def _():
        issue(s + 1, (s + 1) % 2)

    slot = s % 2
    pltpu.make_async_copy(kbuf.at[slot], kbuf.at[slot], sem.at[0, slot]).wait()
    pltpu.make_async_copy(vbuf.at[slot], vbuf.at[slot], sem.at[1, slot]).wait()

    idxr = idxr_ref[...]
    dist = past_len - idxr
    bkt = _bucket_jnp(dist, n_buckets)
    onehot = jnp.where(bkt == lax.broadcasted_iota(I32, (n_buckets, kk), 0), 1.0, 0.0).astype(BF16)
    bias = _dot(rbh_ref[...], onehot) + _dot(rbl_ref[...], onehot)
    is_new = idxc_ref[...] == past_len
    qb = q_ref[...].astype(BF16)
    rows = []
    for h in range(n_heads):
        kh = jnp.where(is_new, knew_ref[h:h + 1, :], kbuf[slot, h])
        rows.append(_dot_nt(qb, kh.astype(BF16))[h:h + 1, :])
    logit = jnp.concatenate(rows, axis=0) * (dh ** -0.5) + bias
    logit = jnp.where(dist >= 0, logit, NEG_INF)
    p = jnp.exp(logit - jnp.max(logit, axis=1, keepdims=True))
    pb = (p / jnp.sum(p, axis=1, keepdims=True)).astype(BF16)
    for h in range(n_heads):
        vh = jnp.where(is_new, vnew_ref[h:h + 1, :], vbuf[slot, h])
        o_ref[h:h + 1, :] = _dot(pb, vh.astype(BF16))[h:h + 1, :]


def _sample_attention(idx3, page_table, q3, rel_bias, knew3, vnew3, cache_k, cache_v, past_len):
    s, kk, _ = idx3.shape
    n_heads, dh = q3.shape[1:]
    page = cache_k.shape[1]
    assert page & (page - 1) == 0
    n_buckets = rel_bias.shape[0]
    rbt = rel_bias.T
    rbh = rbt.astype(BF16)
    rbl = (rbt - rbh.astype(F32)).astype(BF16)
    seq3 = lambda: pl.BlockSpec((None, n_heads, dh), lambda i, a, b: (i, 0, 0))
    grid_spec = pltpu.PrefetchScalarGridSpec(
        num_scalar_prefetch=2,
        grid=(s,),
        in_specs=[seq3(),
                  pl.BlockSpec((None, kk, 1), lambda i, a, b: (i, 0, 0)),
                  pl.BlockSpec((None, 1, kk), lambda i, a, b: (i, 0, 0)),
                  pl.BlockSpec((n_heads, n_buckets), lambda i, a, b: (0, 0)),
                  pl.BlockSpec((n_heads, n_buckets), lambda i, a, b: (0, 0)),
                  seq3(), seq3(),
                  pl.BlockSpec(memory_space=pl.ANY), pl.BlockSpec(memory_space=pl.ANY)],
        out_specs=seq3(),
        scratch_shapes=[pltpu.VMEM((2, n_heads, kk, dh), F32), pltpu.VMEM((2, n_heads, kk, dh), F32),
                        pltpu.SemaphoreType.DMA((2, 2))],
    )
    return pl.pallas_call(
        functools.partial(_sattn_kernel, kk=kk, past_len=past_len, page=page, n_buckets=n_buckets),
        out_shape=_sds((s, n_heads, dh), F32),
        grid_spec=grid_spec,
        compiler_params=_cp(("arbitrary",)),
        name="sample_attention",
    )(idx3.reshape(s, kk), page_table, q3, idx3, idx3.reshape(s, 1, kk), rbh, rbl, knew3, vnew3, cache_k, cache_v)


def kernel(x_prompt, x_sample, cache_k, cache_v, cache_kidx, page_table, state_lru_h, state_lru_conv, state_ffn_conv,
           c_prompt, c_sample, w_ada, b_ada, w_in, lru_conv_w, lru_conv_b, lru_w_a, lru_b_a, lru_w_x, lru_b_x,
           lru_lambda, attn_rel_bias, lru_out_g, attn_out_g, w_out, ln1_g, ln1_b, w_ffn_in, ffn_conv_w, ffn_conv_b,
           w_ffn_out, ln2_g, ln2_b):
    B, T, D = x_prompt.shape
    S, ts, _ = x_sample.shape
    assert ts == 1
    _, page, n_heads, dh = cache_k.shape
    idim = cache_kidx.shape[-1]
    n_pages = page_table.shape[1]
    past_len = n_pages * page
    W = lru_conv_b.shape[0]
    aw = n_heads * dh
    n_idx_heads = (w_in.shape[1] - 2 * W - 3 * aw - idim) // (idim + 1)
    nqi = n_idx_heads * idim
    dff = ffn_conv_b.shape[0]
    assert W % LANES == 0 and dh == LANES and 2 * idim == LANES and S % 8 == 0

    w_in_t = w_in.T
    c0 = 2 * W + 3 * aw
    w_idx = jnp.concatenate([w_in_t[c0:c0 + nqi], w_in_t[c0 + nqi + n_idx_heads:],
                             w_in_t[c0 + nqi:c0 + nqi + n_idx_heads],
                             jnp.zeros((LANES - idim - n_idx_heads, D), F32)], axis=0)
    lru_p = dict(cw=lru_conv_w, cb=lru_conv_b.reshape(1, W),
                 wax=jnp.concatenate([lru_w_a, lru_w_x], axis=2).astype(BF16),
                 ba=lru_b_a.reshape(1, W), bx=lru_b_x.reshape(1, W), lam=lru_lambda.reshape(1, W),
                 g=lru_out_g.reshape(1, W))
    ln1 = (ln1_g.reshape(1, D), ln1_b.reshape(1, D))
    ln2 = (ln2_g.reshape(1, D), ln2_b.reshape(1, D))
    fcw, fcb = ffn_conv_w, ffn_conv_b.reshape(1, dff)
    attn_g = attn_out_g.reshape(1, aw)

    mp = -(-(S + B) // 8) * 8
    c_all = jnp.concatenate([c_sample, c_prompt, jnp.zeros((mp - S - B, D), F32)], axis=0)
    mod = _ada(c_all, w_ada, b_ada)

    st_p = _Stream(B, T, min(1024, T), False, S)
    st_s = _Stream(1, S, S, True, 0)
    xs3 = x_sample.reshape(1, S, D)
    tm = st_p.tm
    MP = B * T

    qi_p, wk_p, w_in_b = _idx_project(st_p, x_prompt, mod, w_idx, nqi, w_in_t, c0)
    qi_s, wk_s = _idx_project(st_s, xs3, mod, w_idx, nqi)
    m1_p = _modulate(st_p, x_prompt, mod, 0, 1).reshape(MP, D)
    m1_s = _modulate(st_s, xs3, mod, 0, 1).reshape(S, D)
    xg_p, xg_s = _matmul_nt(m1_p, m1_s, w_in_b, 0, 2 * W, [F32], [F32], tm)
    q_p, q_s = _matmul_nt(m1_p, m1_s, w_in_b, 2 * W, aw, [BF16], [F32], tm)
    k_p, kb_p, k_s = _matmul_nt(m1_p, m1_s, w_in_b, 2 * W + aw, aw, [F32, BF16], [F32], tm)
    v_p, vb_p, v_s = _matmul_nt(m1_p, m1_s, w_in_b, 2 * W + 2 * aw, aw, [F32, BF16], [F32], tm)
    qi_p, wk_p = qi_p.reshape(MP, nqi), wk_p.reshape(MP, LANES)
    qi_s, wk_s = qi_s.reshape(S, nqi), wk_s.reshape(S, LANES)
    kidx_p, kidx_s = wk_p[:, :idim], wk_s[:, :idim]

    ylru_p, h_p, w_out_b = _lru_prompt(xg_p, B, T, jnp.zeros((B, 8, W), F32), jnp.zeros((B, 1, W), F32), lru_p, w_out)
    conv_p = xg_p.reshape(B, T, 2 * W)[:, T - 3:, :W]
    mask = _prompt_index(qi_p, wk_p, B, T, n_idx_heads, idim, min(TOPK_MAX, T // 4))
    yatt_p, w_ffn_in_b = _prompt_attention(q_p, kb_p, vb_p, mask, attn_rel_bias, B, T, n_heads, dh, w_ffn_in)

    ylru_s, h_s = _lru_sample(xg_s, jnp.swapaxes(state_lru_conv, 0, 1), state_lru_h, lru_p)
    conv_s = jnp.concatenate([state_lru_conv[:, 1:], xg_s[:, None, :W]], axis=1)
    scores = _sample_scores(page_table, qi_s.reshape(S, n_idx_heads, idim),
                            wk_s[:, idim:idim + n_idx_heads].reshape(S, n_idx_heads, 1),
                            kidx_s.reshape(S, 1, idim), jnp.swapaxes(cache_kidx, 1, 2),
                            (n_idx_heads ** -0.5) * (idim ** -0.5))
    idx3 = _sample_select(scores.reshape(S, past_len + LANES), min(TOPK_MAX, (past_len + 1) // 4), past_len + 1)
    yatt_s = _sample_attention(idx3, page_table, q_s.reshape(S, n_heads, dh), attn_rel_bias,
                               k_s.reshape(S, n_heads, dh), v_s.reshape(S, n_heads, dh),
                               cache_k, cache_v, past_len).reshape(S, aw)

    mix_p, mix_s = _matmul_cat(ylru_p, _rms_norm(yatt_p, attn_g, 512), ylru_s, _rms_norm(yatt_s, attn_g, 512),
                               w_out_b, tm)
    x1_p, m2_p = _residual_ln(st_p, x_prompt, mix_p.reshape(B, T, D), mod, 2, *ln1, mod_next=(3, 4))
    x1_s, m2_s = _residual_ln(st_s, xs3, mix_s.reshape(1, S, D), mod, 2, *ln1, mod_next=(3, 4))
    hmid_p, tail, w_ffn_out_b, hmid_s, gate_s = _ffn_in(
        m2_p.reshape(MP, D), m2_s.reshape(S, D), w_ffn_in_b, jnp.zeros((B, 8, dff), F32),
        jnp.swapaxes(state_ffn_conv, 0, 1), fcw, fcb, B, T, w_ffn_out)
    ffn_p = tail[:, 6:, :]
    ffn_s = jnp.concatenate([state_ffn_conv[:, 1:], gate_s[:, None, :]], axis=1)
    f_p, f_s = _matmul_ktiled(hmid_p, hmid_s, w_ffn_out_b, tm, 1024, 4096)
    y_p = _residual_ln(st_p, x1_p, f_p.reshape(B, T, D), mod, 5, *ln2)
    y_s = _residual_ln(st_s, x1_s, f_s.reshape(1, S, D), mod, 5, *ln2)

    return (y_p, y_s.reshape(S, 1, D),
            k_p.reshape(B, T, n_heads, dh), v_p.reshape(B, T, n_heads, dh), kidx_p.reshape(B, T, idim), h_p, conv_p,
            ffn_p,
            k_s.reshape(S, 1, n_heads, dh), v_s.reshape(S, 1, n_heads, dh), kidx_s.reshape(S, 1, idim), h_s, conv_s,
            ffn_s)
```

```python
import functools
import math

import numpy as np
import jax
import jax.numpy as jnp
from jax import lax
from jax.experimental import pallas as pl
from jax.experimental.pallas import tpu as pltpu

F32 = jnp.float32
BF16 = jnp.bfloat16
I32 = jnp.int32

LRU_C = 8.0
TOPK_MAX = 256
MAX_DISTANCE = 128
LN_EPS = 1e-5
DEPTH = 1
DEEPNORM_ALPHA = (2.0 * DEPTH) ** 0.25
LANES = 128
VMEM_LIMIT = 56 * 1024 * 1024
INT_MIN = -(2 ** 31)
NEG_INF = float("-inf")


def _cp(sem, vmem=VMEM_LIMIT):
    return pltpu.CompilerParams(dimension_semantics=sem, vmem_limit_bytes=vmem)


def _sds(shape, dtype):
    return jax.ShapeDtypeStruct(shape, dtype)


def _split_bf16(x):
    hi = x.astype(BF16)
    lo = (x - hi.astype(F32)).astype(BF16)
    return hi, lo


def _dot(a, b):
    return jnp.dot(a, b, preferred_element_type=F32)


def _dot_nt(a, b):
    return lax.dot_general(a, b, (((1,), (1,)), ((), ())), preferred_element_type=F32)


def _dot3(a, b):
    ah, al = _split_bf16(a)
    bh, bl = _split_bf16(b)
    return _dot(ah, bh) + (_dot(al, bh) + _dot(ah, bl))


def _sortable(x):
    b = pltpu.bitcast(x, I32)
    return b ^ ((b >> 31) & 0x7FFFFFFF)


class _Stream:
    def __init__(self, G, R, tm, per_row, mod_row0):
        self.G, self.R, self.tm, self.per_row, self.mod_row0 = G, R, tm, per_row, mod_row0
        self.M = G * R
        self.nr = R // tm

    def mod_operand(self, mod2):
        if self.per_row:
            return mod2
        mp, n6 = mod2.shape
        return mod2.reshape(mp, 6, 1, n6 // 6)

    def mod_spec(self, which, width, d_model, col=lambda *ids: 0):
        if self.per_row:
            nb = d_model // width
            return pl.BlockSpec((self.tm, width), lambda g, i, *r: (i, which * nb + col(g, i, *r)))
        r0 = self.mod_row0
        return pl.BlockSpec((None, None, 1, width), lambda g, i, *r: (r0 + g, which, 0, col(g, i, *r)))


def _ada_kernel(c_ref, w_ref, b_ref, o_ref):
    c = c_ref[...]
    a = c * jax.nn.sigmoid(c)
    o_ref[...] = _dot3(a, w_ref[...]) + b_ref[...]


def _ada(c_all, w_ada, b_ada):
    mp, d = c_all.shape
    n = w_ada.shape[1]
    tn = 1024
    return pl.pallas_call(
        _ada_kernel,
        out_shape=_sds((mp, n), F32),
        grid=(n // tn,),
        in_specs=[pl.BlockSpec((mp, d), lambda j: (0, 0)),
                  pl.BlockSpec((d, tn), lambda j: (0, j)),
                  pl.BlockSpec((1, tn), lambda j: (0, j))],
        out_specs=pl.BlockSpec((mp, tn), lambda j: (0, j)),
        compiler_params=_cp(("arbitrary",)),
        name="ada_mod",
    )(c_all, w_ada, b_ada.reshape(1, n))


def _side_col(nj):
    return lambda i, j, *r: (0, jnp.where(i == 0, j, nj - 1))


def _mm_nt_kernel(a_ref, bt_ref, s_ref, *o_refs, n_main):
    r = _dot_nt(a_ref[...], bt_ref[...])
    for o in o_refs[:n_main]:
        o[...] = r.astype(o.dtype)

    @pl.when(pl.program_id(0) == 0)
    def _():
        r2 = _dot_nt(s_ref[...], bt_ref[...])
        for o in o_refs[n_main:]:
            o[...] = r2.astype(o.dtype)


def _matmul_nt(a, side, bt, col0, n, out_dtypes, side_dtypes, tm, tn=1024):
    m, k = a.shape
    ms = side.shape[0]
    tm = min(tm, m)
    tn = min(tn, n)
    assert m % tm == 0 and n % tn == 0 and col0 % tn == 0
    cb = col0 // tn
    nj = n // tn
    outs = pl.pallas_call(
        functools.partial(_mm_nt_kernel, n_main=len(out_dtypes)),
        out_shape=[_sds((m, n), dt) for dt in out_dtypes] + [_sds((ms, n), dt) for dt in side_dtypes],
        grid=(m // tm, nj),
        in_specs=[pl.BlockSpec((tm, k), lambda i, j: (i, 0)),
                  pl.BlockSpec((tn, k), lambda i, j: (cb + j, 0)),
                  pl.BlockSpec((ms, k), lambda i, j: (0, 0))],
        out_specs=[pl.BlockSpec((tm, tn), lambda i, j: (i, j)) for _ in out_dtypes]
        + [pl.BlockSpec((ms, tn), _side_col(nj)) for _ in side_dtypes],
        compiler_params=_cp(("arbitrary", "arbitrary")),
        name="matmul_nt",
    )(a, bt, side)
    return outs


def _mm2_kernel(a1_ref, a2_ref, b_ref, s1_ref, s2_ref, o_ref, os_ref):
    k1 = a1_ref.shape[1]
    o_ref[...] = _dot(a1_ref[...], b_ref[0:k1, :]) + _dot(a2_ref[...], b_ref[k1:, :])

    @pl.when(pl.program_id(0) == 0)
    def _():
        os_ref[...] = _dot(s1_ref[...], b_ref[0:k1, :]) + _dot(s2_ref[...], b_ref[k1:, :])


def _matmul_cat(a1, a2, s1, s2, b, tm, tn=1024):
    m, k1 = a1.shape
    k2 = a2.shape[1]
    ms = s1.shape[0]
    n = b.shape[1]
    tm = min(tm, m)
    nj = n // tn
    return pl.pallas_call(
        _mm2_kernel,
        out_shape=[_sds((m, n), F32), _sds((ms, n), F32)],
        grid=(m // tm, nj),
        in_specs=[pl.BlockSpec((tm, k1), lambda i, j: (i, 0)),
                  pl.BlockSpec((tm, k2), lambda i, j: (i, 0)),
                  pl.BlockSpec((k1 + k2, tn), lambda i, j: (0, j)),
                  pl.BlockSpec((ms, k1), lambda i, j: (0, 0)),
                  pl.BlockSpec((ms, k2), lambda i, j: (0, 0))],
        out_specs=[pl.BlockSpec((tm, tn), lambda i, j: (i, j)), pl.BlockSpec((ms, tn), _side_col(nj))],
        compiler_params=_cp(("arbitrary", "arbitrary")),
        name="matmul_cat",
    )(a1, a2, b, s1, s2)


def _mmk_kernel(a_ref, b_ref, s_ref, o_ref, os_ref):
    first_k = pl.program_id(2) == 0
    part = _dot(a_ref[...], b_ref[...])

    @pl.when(first_k)
    def _():
        o_ref[...] = part

    @pl.when(jnp.logical_not(first_k))
    def _():
        o_ref[...] += part

    @pl.when(pl.program_id(0) == 0)
    def _():
        part_s = _dot(s_ref[...], b_ref[...])

        @pl.when(first_k)
        def _():
            os_ref[...] = part_s

        @pl.when(jnp.logical_not(first_k))
        def _():
            os_ref[...] += part_s


def _matmul_ktiled(a, side, b, tm, tn, tk):
    m, k = a.shape
    ms = side.shape[0]
    n = b.shape[1]
    tm = min(tm, m)
    nj = n // tn
    return pl.pallas_call(
        _mmk_kernel,
        out_shape=[_sds((m, n), F32), _sds((ms, n), F32)],
        grid=(m // tm, nj, k // tk),
        in_specs=[pl.BlockSpec((tm, tk), lambda i, j, kk: (i, kk)),
                  pl.BlockSpec((tk, tn), lambda i, j, kk: (kk, j)),
                  pl.BlockSpec((ms, tk), lambda i, j, kk: (0, kk))],
        out_specs=[pl.BlockSpec((tm, tn), lambda i, j, kk: (i, j)), pl.BlockSpec((ms, tn), _side_col(nj))],
        compiler_params=_cp(("arbitrary", "arbitrary", "arbitrary")),
        name="matmul_ktiled",
    )(a, b, side)


def _idxproj_kernel(x_ref, sh_ref, sc_ref, w_ref, *refs, nq, rider):
    if rider:
        wsrc_ref, q_ref, wk_ref, m_ref, wdst_ref, acc_ref = refs
        wdst_ref[...] = wsrc_ref[...].astype(wdst_ref.dtype)
    else:
        q_ref, wk_ref, m_ref, acc_ref = refs
    kk = pl.program_id(2)
    m = x_ref[...] * (1.0 + sc_ref[...]) + sh_ref[...]
    mh, ml = _split_bf16(m)
    m_ref[...] = mh
    wh, wl = _split_bf16(w_ref[...])
    part = _dot_nt(mh, wh) + (_dot_nt(ml, wh) + _dot_nt(mh, wl))

    @pl.when(kk == 0)
    def _():
        acc_ref[...] = part

    @pl.when(kk > 0)
    def _():
        acc_ref[...] += part

    @pl.when(kk == pl.num_programs(2) - 1)
    def _():
        q_ref[...] = acc_ref[:, 0:nq]
        wk_ref[...] = acc_ref[:, nq:]


def _idx_project(st, x3, mod2, w_idx, nq, w_cast=None, cast_rows=0):
    d = x3.shape[-1]
    nw = w_idx.shape[0]
    tm = min(st.tm, 512)
    tk = 1024
    ni, nk = st.R // tm, d // tk
    st2 = _Stream(st.G, st.R, tm, st.per_row, st.mod_row0)
    modop = st2.mod_operand(mod2)
    kcol = lambda g, i, kk: kk
    in_specs = [pl.BlockSpec((None, tm, tk), lambda g, i, kk: (g, i, kk)),
                st2.mod_spec(0, tk, d, kcol), st2.mod_spec(1, tk, d, kcol),
                pl.BlockSpec((nw, tk), lambda g, i, kk: (0, kk))]
    out_shape = [_sds((st.G, st.R, nq), F32), _sds((st.G, st.R, nw - nq), F32), _sds((st.G, st.R, d), BF16)]
    out_specs = [pl.BlockSpec((None, tm, nq), lambda g, i, kk: (g, i, 0)),
                 pl.BlockSpec((None, tm, nw - nq), lambda g, i, kk: (g, i, 0)),
                 pl.BlockSpec((None, tm, tk), lambda g, i, kk: (g, i, kk))]
    operands = [x3, modop, modop, w_idx]
    if w_cast is not None:
        rc = _rider_rows(cast_rows, st.G * ni * nk)
        slab = pl.BlockSpec((rc, w_cast.shape[1]), lambda g, i, kk: ((g * ni + i) * nk + kk, 0))
        in_specs.append(slab)
        out_shape.append(_sds((cast_rows, w_cast.shape[1]), BF16))
        out_specs.append(slab)
        operands.append(w_cast)
    return pl.pallas_call(
        functools.partial(_idxproj_kernel, nq=nq, rider=w_cast is not None),
        out_shape=out_shape,
        grid=(st.G, ni, nk),
        in_specs=in_specs,
        out_specs=out_specs,
        scratch_shapes=[pltpu.VMEM((tm, nw), F32)],
        compiler_params=_cp(("arbitrary", "arbitrary", "arbitrary")),
        name="idx_project",
    )(*operands)


def _softplus(z):
    return jnp.maximum(z, 0.0) + jnp.log1p(jnp.exp(-jnp.abs(z)))


def _neg_expm1(x):
    poly = x * (-1.0 + x * (-1.0 / 2 + x * (-1.0 / 6 + x * (-1.0 / 24))))
    return jnp.where(x > -1.0 / 16, poly, 1.0 - jnp.exp(x))


def _lru_gates(xc, wax, ba, bx, lam):
    ri = _dot(xc.astype(BF16), wax)
    r = jax.nn.sigmoid(ri[:, :LANES] + ba)
    ig = jax.nn.sigmoid(ri[:, LANES:] + bx)
    log_a = (-LRU_C * r) * _softplus(-lam)
    a = jnp.exp(log_a)
    u = jnp.sqrt(_neg_expm1(2.0 * log_a)) * (ig * xc)
    return a, u


def _lru_prompt_kernel(xr_ref, gr_ref, hist_ref, h0_ref, cw_ref, cb_ref, wax_ref, ba_ref, bx_ref, lam_ref, g_ref,
                       wsrc_ref, y_ref, hlast_ref, wdst_ref, xbuf, hcar, ybuf):
    wdst_ref[...] = wsrc_ref[...].astype(wdst_ref.dtype)
    t = pl.program_id(1)
    tc, w = xr_ref.shape
    nb = w // LANES

    @pl.when(t == 0)
    def _():
        xbuf[0:8, :] = hist_ref[...]
        hcar[...] = h0_ref[...]

    @pl.when(t > 0)
    def _():
        xbuf[0:8, :] = xbuf[tc:tc + 8, :]

    xbuf[8:8 + tc, :] = xr_ref[...]
    rows3 = lax.broadcasted_iota(I32, (tc // 8, 8, LANES), 1)
    ssq = jnp.zeros((tc, 1), F32)
    for n in range(nb):
        sl = slice(n * LANES, (n + 1) * LANES)
        z3 = xbuf[0:8 + tc, sl].reshape(tc // 8 + 1, 8, LANES)
        xc = cb_ref[:, sl]
        for d in (3, 2, 1):
            rot = pltpu.roll(z3, d, axis=1)
            xc = xc + jnp.where(rows3 >= d, rot[1:], rot[:-1]).reshape(tc, LANES) * cw_ref[3 - d:4 - d, sl]
        xc = xc + z3[1:].reshape(tc, LANES) * cw_ref[3:4, sl]
        a, u = _lru_gates(xc, wax_ref[n], ba_ref[:, sl], bx_ref[:, sl], lam_ref[:, sl])
        a3 = a.reshape(tc // 8, 8, LANES)
        u3 = u.reshape(tc // 8, 8, LANES)
        for s in (1, 2, 4):
            keep = rows3 >= s
            u3 = jnp.where(keep, a3 * pltpu.roll(u3, s, axis=1) + u3, u3)
            a3 = jnp.where(keep, a3 * pltpu.roll(a3, s, axis=1), a3)
        h_in = hcar[:, sl]
        hs = []
        for g in range(tc // 8):
            hg = a3[g] * h_in + u3[g]
            hs.append(hg)
            h_in = hg[7:8, :]
        h = jnp.concatenate(hs, axis=0)
        hcar[:, sl] = h_in
        y = h * jax.nn.gelu(gr_ref[:, sl])
        ybuf[:, sl] = y
        ssq = ssq + jnp.sum(y * y, axis=1, keepdims=True)
    scale = lax.rsqrt(ssq * (1.0 / w) + LN_EPS)
    y_ref[...] = (ybuf[...] * scale * g_ref[...]).astype(y_ref.dtype)
    hlast_ref[...] = hcar[...]


def _lru_prompt(xg, B, T, hist8, h0, p, w_cast):
    w = xg.shape[1] // 2
    tc = min(256, T)
    nt = T // tc
    rc = _rider_rows(w_cast.shape[0], B * nt)
    wn = w_cast.shape[1]
    vec = lambda: pl.BlockSpec((1, w), lambda b, t: (0, 0))
    y, hl, w_b = pl.pallas_call(
        _lru_prompt_kernel,
        out_shape=[_sds((B * T, w), BF16), _sds((B, 1, w), F32), _sds(w_cast.shape, BF16)],
        grid=(B, nt),
        in_specs=[pl.BlockSpec((tc, w), lambda b, t: (b * nt + t, 0)),
                  pl.BlockSpec((tc, w), lambda b, t: (b * nt + t, 1)),
                  pl.BlockSpec((None, 8, w), lambda b, t: (b, 0, 0)),
                  pl.BlockSpec((None, 1, w), lambda b, t: (b, 0, 0)),
                  pl.BlockSpec((4, w), lambda b, t: (0, 0)), vec(),
                  pl.BlockSpec(p["wax"].shape, lambda b, t: (0, 0, 0)),
                  vec(), vec(), vec(), vec(),
                  pl.BlockSpec((rc, wn), lambda b, t: (b * nt + t, 0))],
        out_specs=[pl.BlockSpec((tc, w), lambda b, t: (b * nt + t, 0)),
                   pl.BlockSpec((None, 1, w), lambda b, t: (b, 0, 0)),
                   pl.BlockSpec((rc, wn), lambda b, t: (b * nt + t, 0))],
        scratch_shapes=[pltpu.VMEM((tc + 8, w), F32), pltpu.VMEM((1, w), F32), pltpu.VMEM((tc, w), F32)],
        compiler_params=_cp(("arbitrary", "arbitrary")),
        name="rglru_prompt",
    )(xg, xg, hist8, h0, p["cw"], p["cb"], p["wax"], p["ba"], p["bx"], p["lam"], p["g"], w_cast)
    return y, hl.reshape(B, w), w_b


def _lru_sample_kernel(xr_ref, gr_ref, hist_ref, h0_ref, cw_ref, cb_ref, wax_ref, ba_ref, bx_ref, lam_ref, g_ref,
                       y_ref, h_ref, ybuf):
    r, w = xr_ref.shape
    nb = w // LANES
    ssq = jnp.zeros((r, 1), F32)
    for n in range(nb):
        sl = slice(n * LANES, (n + 1) * LANES)
        xc = cb_ref[:, sl]
        for j in range(3):
            xc = xc + hist_ref[j, :, sl] * cw_ref[j:j + 1, sl]
        xc = xc + xr_ref[:, sl] * cw_ref[3:4, sl]
        a, u = _lru_gates(xc, wax_ref[n], ba_ref[:, sl], bx_ref[:, sl], lam_ref[:, sl])
        h = a * h0_ref[:, sl] + u
        h_ref[:, sl] = h
        y = h * jax.nn.gelu(gr_ref[:, sl])
        ybuf[:, sl] = y
        ssq = ssq + jnp.sum(y * y, axis=1, keepdims=True)
    scale = lax.rsqrt(ssq * (1.0 / w) + LN_EPS)
    y_ref[...] = (ybuf[...] * scale * g_ref[...]).astype(y_ref.dtype)


def _lru_sample(xg, hist_t, h0, p):
    s = xg.shape[0]
    w = xg.shape[1] // 2
    vec = lambda: pl.BlockSpec((1, w), lambda i: (0, 0))
    return pl.pallas_call(
        _lru_sample_kernel,
        out_shape=[_sds((s, w), BF16), _sds((s, w), F32)],
        grid=(1,),
        in_specs=[pl.BlockSpec((s, w), lambda i: (0, 0)), pl.BlockSpec((s, w), lambda i: (0, 1)),
                  pl.BlockSpec((3, s, w), lambda i: (0, 0, 0)), pl.BlockSpec((s, w), lambda i: (0, 0)),
                  pl.BlockSpec((4, w), lambda i: (0, 0)), vec(),
                  pl.BlockSpec(p["wax"].shape, lambda i: (0, 0, 0)),
                  vec(), vec(), vec(), vec()],
        out_specs=[pl.BlockSpec((s, w), lambda i: (0, 0)), pl.BlockSpec((s, w), lambda i: (0, 0))],
        scratch_shapes=[pltpu.VMEM((s, w), F32)],
        compiler_params=_cp(("arbitrary",)),
        name="rglru_sample",
    )(xg, xg, hist_t, h0, p["cw"], p["cb"], p["wax"], p["ba"], p["bx"], p["lam"], p["g"])


def _kth_largest(count_ge, kk, rows):
    def body(p, thr):
        bit = jnp.left_shift(jnp.int32(1), 31 - p)
        trial = thr + bit
        return jnp.where(count_ge(trial) >= kk, trial, thr)
    return lax.fori_loop(0, 32, body, jnp.full((rows, 1), INT_MIN, I32))


def _pidx_kernel(q_ref, wkq_ref, wkall_ref, o_ref, kcat, qcat, wb, keys, *, n_heads, idim, kk, kc):
    i = pl.program_id(1)
    tq = q_ref.shape[0]
    t_all = wkall_ref.shape[0]
    lane = lax.broadcasted_iota(I32, (tq, LANES), 1)

    @pl.when(i == 0)
    def _():
        k = wkall_ref[...]
        kh = k.astype(BF16).astype(F32)
        kl = k - kh
        left = lax.broadcasted_iota(I32, k.shape, 1) < idim
        kcat[:, 0:LANES] = jnp.where(left, kh, pltpu.roll(kl, idim, axis=1)).astype(BF16)
        kcat[:, LANES:] = jnp.where(left, kh, 0.0).astype(BF16)

    for pr in range(n_heads // 2):
        v = q_ref[:, pr * LANES:(pr + 1) * LANES]
        vh = v.astype(BF16).astype(F32)
        vl = v - vh
        vh_r = pltpu.roll(vh, idim, axis=1)
        vl_r = pltpu.roll(vl, idim, axis=1)
        first = lane < idim
        qcat[2 * pr, :, 0:LANES] = jnp.where(first, vh, vh_r).astype(BF16)
        qcat[2 * pr, :, LANES:] = jnp.where(first, vl, 0.0).astype(BF16)
        qcat[2 * pr + 1, :, 0:LANES] = jnp.where(first, vh_r, vh).astype(BF16)
        qcat[2 * pr + 1, :, LANES:] = jnp.where(first, vl_r, 0.0).astype(BF16)
    scale = (n_heads ** -0.5) * (idim ** -0.5)
    wq = wkq_ref[...] * scale
    for h in range(n_heads):
        wb[h] = jnp.broadcast_to(wq[:, idim + h:idim + h + 1], (tq, LANES))

    nch = (i * tq + tq + kc - 1) // kc
    qpos = i * tq + lax.broadcasted_iota(I32, (tq, kc), 0)
    cols = lax.broadcasted_iota(I32, (tq, kc), 1)

    def score_chunk(c, carry):
        k_c = kcat[pl.ds(pl.multiple_of(c * kc, kc), kc), :]
        x_all = _dot_nt(qcat[...].reshape(n_heads * tq, 2 * LANES), k_c)
        acc = jnp.zeros((tq, kc), F32)
        for h in range(n_heads):
            acc = acc + jnp.maximum(x_all[h * tq:(h + 1) * tq], 0.0) * jnp.tile(wb[h], (1, kc // LANES))
        key = jnp.where(c * kc + cols <= qpos, _sortable(acc + 0.0), INT_MIN)
        keys[:, pl.ds(pl.multiple_of(c * kc, kc), kc)] = key
        return carry

    lax.fori_loop(0, nch, score_chunk, 0)

    def count_where(preds):
        def body(c, cnts):
            kv = keys[:, pl.ds(pl.multiple_of(c * kc, kc), kc)]
            return [cnt + _fold_lanes(pred(kv).astype(I32), jnp.add) for cnt, pred in zip(cnts, preds)]
        cnts = lax.fori_loop(0, nch, body, [jnp.zeros((tq, LANES), I32) for _ in preds])
        return [jnp.sum(cnt, axis=1, keepdims=True) for cnt in cnts]

    thr = _kth_largest(lambda t: count_where([lambda kv: kv >= t])[0], kk, tq)
    n_ge, = count_where([lambda kv: (kv >= thr) & (kv > INT_MIN)])
    tie = jnp.max(n_ge) > kk

    o_ref[...] = jnp.full(o_ref.shape, NEG_INF, F32)

    @pl.when(jnp.logical_not(tie))
    def _():
        def body(c, carry):
            sl = pl.ds(pl.multiple_of(c * kc, kc), kc)
            kv = keys[:, sl]
            o_ref[:, sl] = jnp.where((kv >= thr) & (kv > INT_MIN), 0.0, NEG_INF)
            return carry
        lax.fori_loop(0, nch, body, 0)

    @pl.when(tie)
    def _():
        n_gt, = count_where([lambda kv: kv > thr])
        need = (kk - n_gt).astype(F32)
        tri = (lax.broadcasted_iota(I32, (kc, kc), 0) < lax.broadcasted_iota(I32, (kc, kc), 1)).astype(BF16)

        def body(c, seen):
            sl = pl.ds(pl.multiple_of(c * kc, kc), kc)
            kv = keys[:, sl]
            eq = (kv == thr) & (kv > INT_MIN)
            eqf = jnp.where(eq, 1.0, 0.0)
            before = seen + _dot(eqf.astype(BF16), tri)
            sel = (kv > thr) | (eq & (before < need))
            o_ref[:, sl] = jnp.where(sel, 0.0, NEG_INF)
            return seen + jnp.sum(eqf, axis=1, keepdims=True)
        lax.fori_loop(0, nch, body, jnp.zeros((tq, 1), F32))


def _prompt_index(qi, wk, B, T, n_heads, idim, kk):
    tq = min(256, T)
    kc = min(512, T)
    nq = T // tq
    return pl.pallas_call(
        functools.partial(_pidx_kernel, n_heads=n_heads, idim=idim, kk=kk, kc=kc),
        out_shape=_sds((B * T, T), F32),
        grid=(B, nq),
        in_specs=[pl.BlockSpec((tq, n_heads * idim), lambda b, i: (b * nq + i, 0)),
                  pl.BlockSpec((tq, LANES), lambda b, i: (b * nq + i, 0)),
                  pl.BlockSpec((T, LANES), lambda b, i: (b, 0))],
        out_specs=pl.BlockSpec((tq, T), lambda b, i: (b * nq + i, 0)),
        scratch_shapes=[pltpu.VMEM((T, 2 * LANES), BF16), pltpu.VMEM((n_heads, tq, 2 * LANES), BF16),
                        pltpu.VMEM((n_heads, tq, LANES), F32), pltpu.VMEM((tq, T), I32)],
        compiler_params=_cp(("arbitrary", "arbitrary")),
        name="prompt_index",
    )(qi, wk, wk)


def _bucket_np(d):
    n_buckets = 32
    max_exact = n_buckets // 2
    d = np.maximum(d, 0)
    large = max_exact + (np.log(np.maximum(d, 1).astype(np.float32) / np.float32(max_exact))
                         / np.float32(math.log(MAX_DISTANCE / max_exact))
                         * np.float32(n_buckets - max_exact)).astype(np.int32)
    large = np.minimum(large, n_buckets - 1)
    return np.where(d < max_exact, d, large).astype(np.int32)


def _bucket_jnp(d, n_buckets):
    max_exact = n_buckets // 2
    d = jnp.maximum(d, 0)
    large = max_exact + (jnp.log(jnp.maximum(d, 1).astype(F32) / max_exact)
                         / math.log(MAX_DISTANCE / max_exact) * (n_buckets - max_exact)).astype(I32)
    large = jnp.minimum(large, n_buckets - 1)
    return jnp.where(d < max_exact, d, large)


def _fold_lanes(x, op):
    out = x[:, 0:LANES]
    for s in range(1, x.shape[1] // LANES):
        out = op(out, x[:, s * LANES:(s + 1) * LANES])
    return out


def _pattn_kernel(rb_ref, q_ref, k_ref, v_ref, mask_ref, bkt_ref, wsrc_ref, o_ref, wdst_ref, tbl, sbuf, mx_s, l_s, acc_s,
                  *, hp, kc, n_far):
    wdst_ref[...] = wsrc_ref[...].astype(wdst_ref.dtype)
    hg = pl.program_id(1)
    i = pl.program_id(2)
    tq = q_ref.shape[0]
    dh = q_ref.shape[1] // hp
    log2e = math.log2(math.e)
    c_exp = dh ** -0.5 * log2e
    n_buckets = rb_ref.shape[0]

    @pl.when(i == 0)
    def _():
        tbl[...] = jnp.zeros(tbl.shape, F32)
        bkt = bkt_ref[:, kc - tq:kc + tq]
        for h in range(hp):
            far_b = rb_ref[n_far, hg * hp + h]
            acc = jnp.full(bkt.shape, far_b, F32)
            for b in range(n_buckets):
                acc = jnp.where(bkt == b, rb_ref[b, hg * hp + h], acc)
            tbl[h, :, kc - tq:kc + tq] = (acc - far_b) * log2e

    for h in range(hp):
        mx_s[h] = jnp.full((tq, LANES), NEG_INF, F32)
        l_s[h] = jnp.zeros((tq, LANES), F32)
        acc_s[h] = jnp.zeros((tq, dh), F32)

    nch = (i * tq + tq + kc - 1) // kc
    c_near = jnp.maximum((i * tq - tq) // kc, 0)

    def logits(c, bias_of):
        sl = pl.ds(pl.multiple_of(c * kc, kc), kc)
        msk = mask_ref[:, sl]
        for h in range(hp):
            hs = slice(h * dh, (h + 1) * dh)
            s = bias_of(h, _dot_nt(q_ref[:, hs], k_ref[sl, hs]) * c_exp + msk)
            sbuf[h, :, sl] = s
            mx_s[h] = jnp.maximum(mx_s[h], _fold_lanes(s, jnp.maximum))

    def far(c, carry):
        logits(c, lambda h, s: s)
        return carry

    def near(c, carry):
        off = pl.multiple_of(kc - (i * tq - c * kc), LANES)
        logits(c, lambda h, s: s + tbl[h, :, pl.ds(off, kc)])
        return carry

    lax.fori_loop(0, c_near, far, 0)
    lax.fori_loop(c_near, nch, near, 0)
    for h in range(hp):
        mx_s[h] = jnp.broadcast_to(jnp.max(mx_s[h], axis=1, keepdims=True), (tq, LANES))

    ones = jnp.ones((kc, LANES), BF16)

    def weigh(c, carry):
        sl = pl.ds(pl.multiple_of(c * kc, kc), kc)
        for h in range(hp):
            hs = slice(h * dh, (h + 1) * dh)
            p = jnp.exp2(sbuf[h, :, sl] - jnp.tile(mx_s[h], (1, kc // LANES))).astype(BF16)
            r = _dot(p, jnp.concatenate([v_ref[sl, hs], ones], axis=1))
            acc_s[h] = acc_s[h] + r[:, :dh]
            l_s[h] = l_s[h] + r[:, dh:]
        return carry

    lax.fori_loop(0, nch, weigh, 0)
    for h in range(hp):
        o_ref[:, h * dh:(h + 1) * dh] = acc_s[h] / l_s[h]


def _prompt_attention(q, k, v, mask, rel_bias, B, T, n_heads, dh, w_cast):
    tq = 128
    kc = min(1024, T)
    hp = 4
    nq = T // tq
    ng = n_heads // hp
    rc = _rider_rows(w_cast.shape[0], B * ng * nq)
    wn = w_cast.shape[1]
    n_buckets = rel_bias.shape[0]
    r = np.arange(tq)[:, None]
    x = np.arange(2 * kc)[None, :]
    dist = r + kc - x
    bkt = np.where(dist >= 0, _bucket_np(dist), -1).astype(np.int32)
    far_d = kc + tq
    assert _bucket_np(np.array([tq + 1]))[0] == n_buckets - 1 and far_d > tq
    grid_spec = pltpu.PrefetchScalarGridSpec(
        num_scalar_prefetch=0,
        grid=(B, n_heads // hp, nq),
        in_specs=[pl.BlockSpec(memory_space=pltpu.SMEM),
                  pl.BlockSpec((tq, hp * dh), lambda b, g, i: (b * nq + i, g)),
                  pl.BlockSpec((T, hp * dh), lambda b, g, i: (b, g)),
                  pl.BlockSpec((T, hp * dh), lambda b, g, i: (b, g)),
                  pl.BlockSpec((tq, T), lambda b, g, i: (b * nq + i, 0)),
                  pl.BlockSpec((tq, 2 * kc), lambda b, g, i: (0, 0)),
                  pl.BlockSpec((rc, wn), lambda b, g, i: ((b * ng + g) * nq + i, 0))],
        out_specs=[pl.BlockSpec((tq, hp * dh), lambda b, g, i: (b * nq + i, g)),
                   pl.BlockSpec((rc, wn), lambda b, g, i: ((b * ng + g) * nq + i, 0))],
        scratch_shapes=[pltpu.VMEM((hp, tq, 2 * kc), F32), pltpu.VMEM((hp, tq, T), F32),
                        pltpu.VMEM((hp, tq, LANES), F32), pltpu.VMEM((hp, tq, LANES), F32),
                        pltpu.VMEM((hp, tq, dh), F32)],
    )
    return pl.pallas_call(
        functools.partial(_pattn_kernel, hp=hp, kc=kc, n_far=n_buckets - 1),
        out_shape=[_sds((B * T, n_heads * dh), F32), _sds(w_cast.shape, BF16)],
        grid_spec=grid_spec,
        compiler_params=_cp(("arbitrary", "arbitrary", "arbitrary")),
        name="prompt_attention",
    )(rel_bias, q, k, v, mask, jnp.asarray(bkt), w_cast)


def _rms_kernel(x_ref, g_ref, o_ref):
    x = x_ref[...]
    ms = jnp.mean(x * x, axis=-1, keepdims=True)
    o_ref[...] = (x * lax.rsqrt(ms + LN_EPS) * g_ref[...]).astype(o_ref.dtype)


def _rms_norm(x, g, tm):
    m, w = x.shape
    tm = min(tm, m)
    return pl.pallas_call(
        _rms_kernel,
        out_shape=_sds((m, w), BF16),
        grid=(m // tm,),
        in_specs=[pl.BlockSpec((tm, w), lambda i: (i, 0)), pl.BlockSpec((1, w), lambda i: (0, 0))],
        out_specs=pl.BlockSpec((tm, w), lambda i: (i, 0)),
        compiler_params=_cp(("arbitrary",)),
        name="rms_norm",
    )(x, g)


def _layer_norm(z, g, b):
    mu = jnp.mean(z, axis=-1, keepdims=True)
    zc = z - mu
    var = jnp.mean(zc * zc, axis=-1, keepdims=True)
    return zc * lax.rsqrt(var + LN_EPS) * g + b


def _ln_mod_kernel(x_ref, f_ref, gate_ref, sh_ref, sc_ref, g_ref, b_ref, x1_ref, m_ref):
    z = DEEPNORM_ALPHA * x_ref[...] + gate_ref[...] * f_ref[...]
    x1 = _layer_norm(z, g_ref[...], b_ref[...])
    x1_ref[...] = x1
    m_ref[...] = (x1 * (1.0 + sc_ref[...]) + sh_ref[...]).astype(m_ref.dtype)


def _ln_kernel(x_ref, f_ref, gate_ref, g_ref, b_ref, y_ref):
    z = DEEPNORM_ALPHA * x_ref[...] + gate_ref[...] * f_ref[...]
    y_ref[...] = _layer_norm(z, g_ref[...], b_ref[...])


def _residual_ln(st, x3, f3, mod2, gate_which, ln_g, ln_b, mod_next=None):
    d = x3.shape[-1]
    tm = min(st.tm, 256)
    st2 = _Stream(st.G, st.R, tm, st.per_row, st.mod_row0)
    modop = st2.mod_operand(mod2)
    row = pl.BlockSpec((None, tm, d), lambda g, i: (g, i, 0))
    vec = pl.BlockSpec((1, d), lambda g, i: (0, 0))
    if mod_next is None:
        return pl.pallas_call(
            _ln_kernel,
            out_shape=_sds(x3.shape, F32),
            grid=(st.G, st.R // tm),
            in_specs=[row, row, st2.mod_spec(gate_which, d, d), vec, vec],
            out_specs=row,
            compiler_params=_cp(("arbitrary", "arbitrary")),
            name="residual_ln",
        )(x3, f3, modop, ln_g, ln_b)
    return pl.pallas_call(
        _ln_mod_kernel,
        out_shape=[_sds(x3.shape, F32), _sds(x3.shape, BF16)],
        grid=(st.G, st.R // tm),
        in_specs=[row, row, st2.mod_spec(gate_which, d, d), st2.mod_spec(mod_next[0], d, d),
                  st2.mod_spec(mod_next[1], d, d), vec, vec],
        out_specs=[row, row],
        compiler_params=_cp(("arbitrary", "arbitrary")),
        name="residual_ln_mod",
    )(x3, f3, modop, modop, modop, ln_g, ln_b)


def _rider_rows(n_rows, n_steps):
    rows = n_rows // n_steps
    assert rows * n_steps == n_rows and rows % 16 == 0, (n_rows, n_steps)
    return rows


def _ffn_in_seq_kernel(a_ref, wg_ref, wu_ref, hist_ref, cw_ref, cb_ref, wsrc_ref, s_ref, shist_ref,
                       h_ref, tail_ref, wdst_ref, hs_ref, gs_ref, carry, *, rows_per_seq):
    wdst_ref[...] = wsrc_ref[...].astype(wdst_ref.dtype)
    i = pl.program_id(1)

    @pl.when(i == 0)
    def _():
        gate = _dot(s_ref[...], wg_ref[...])
        up = _dot(s_ref[...], wu_ref[...])
        gc = cb_ref[...] + shist_ref[0] * cw_ref[0:1, :] + shist_ref[1] * cw_ref[1:2, :] + gate * cw_ref[2:3, :]
        hs_ref[...] = (jax.nn.gelu(gc) * up).astype(hs_ref.dtype)
        gs_ref[...] = gate

    tm = a_ref.shape[0]
    tiles_per_seq = rows_per_seq // tm
    tn = wg_ref.shape[1]
    cn = min(tn, 2 * LANES)

    @pl.when(i % tiles_per_seq == 0)
    def _():
        carry[...] = hist_ref[...]

    rows = lax.broadcasted_iota(I32, (tm, cn), 0)
    for c in range(tn // cn):
        cs = slice(c * cn, (c + 1) * cn)
        gate = _dot(a_ref[...], wg_ref[:, cs])
        up = _dot(a_ref[...], wu_ref[:, cs])
        prev = carry[:, cs]
        g1 = jnp.where(rows >= 1, pltpu.roll(gate, 1, axis=0), prev[7:8, :])
        g2 = jnp.where(rows >= 2, pltpu.roll(gate, 2, axis=0), jnp.where(rows == 1, prev[7:8, :], prev[6:7, :]))
        gc = cb_ref[:, cs] + g2 * cw_ref[0:1, cs] + g1 * cw_ref[1:2, cs] + gate * cw_ref[2:3, cs]
        h_ref[:, cs] = (jax.nn.gelu(gc) * up).astype(h_ref.dtype)
        carry[:, cs] = gate[tm - 8:tm, :]
        tail_ref[:, cs] = gate[tm - 8:tm, :]


def _ffn_in(m2, side, w_ffn_in, hist8, side_hist_t, cw, cb, B, T, w_cast):
    mt, d = m2.shape
    ms = side.shape[0]
    dff = w_ffn_in.shape[1] // 2
    tm = min(1024, T)
    tn = 512
    nj = dff // tn
    ni = mt // tm
    tps = T // tm
    rc = _rider_rows(w_cast.shape[0], nj * ni)
    wn = w_cast.shape[1]
    return pl.pallas_call(
        functools.partial(_ffn_in_seq_kernel, rows_per_seq=T),
        out_shape=[_sds((mt, dff), BF16), _sds((B, 8, dff), F32), _sds(w_cast.shape, BF16),
                   _sds((ms, dff), BF16), _sds((ms, dff), F32)],
        grid=(nj, ni),
        in_specs=[pl.BlockSpec((tm, d), lambda j, i: (i, 0)),
                  pl.BlockSpec((d, tn), lambda j, i: (0, j)),
                  pl.BlockSpec((d, tn), lambda j, i: (0, nj + j)),
                  pl.BlockSpec((None, 8, tn), lambda j, i: (i // tps, 0, j)),
                  pl.BlockSpec((3, tn), lambda j, i: (0, j)),
                  pl.BlockSpec((1, tn), lambda j, i: (0, j)),
                  pl.BlockSpec((rc, wn), lambda j, i: (j * ni + i, 0)),
                  pl.BlockSpec((ms, d), lambda j, i: (0, 0)),
                  pl.BlockSpec((2, ms, tn), lambda j, i: (0, 0, j))],
        out_specs=[pl.BlockSpec((tm, tn), lambda j, i: (i, j)),
                   pl.BlockSpec((None, 8, tn), lambda j, i: (i // tps, 0, j)),
                   pl.BlockSpec((rc, wn), lambda j, i: (j * ni + i, 0)),
                   pl.BlockSpec((ms, tn), lambda j, i: (0, j)),
                   pl.BlockSpec((ms, tn), lambda j, i: (0, j))],
        scratch_shapes=[pltpu.VMEM((8, tn), F32)],
        compiler_params=_cp(("arbitrary", "arbitrary")),
        name="ffn_in",
    )(m2, w_ffn_in, w_ffn_in, hist8, cw, cb, w_cast, side, side_hist_t)


def _sscore_kernel(pt_ref, q_ref, w_ref, knew_ref, cache_ref, o_ref, kbuf, sem, *, n_pages, scale):
    s = pl.program_id(0)
    ns = pl.num_programs(0)
    page = kbuf.shape[3]

    def copies(seq, slot):
        return [pltpu.make_async_copy(cache_ref.at[pt_ref[seq, pg]], kbuf.at[slot, pg], sem.at[slot])
                for pg in range(n_pages)]

    @pl.when(s == 0)
    def _():
        for cp in copies(0, 0):
            cp.start()

    @pl.when(s + 1 < ns)
    def _():
        for cp in copies(s + 1, (s + 1) % 2):
            cp.start()

    slot = s % 2
    for cp in copies(s, slot):
        cp.wait()

    q = q_ref[...]
    qh = q.astype(BF16).astype(F32)
    q3 = jnp.concatenate([qh, q - qh, qh], axis=1).astype(BF16)
    w = w_ref[...] * scale
    for pg in range(n_pages):
        kt = kbuf[slot, pg]
        kh, kl = _split_bf16(kt)
        x = _dot(q3, jnp.concatenate([kh, kh, kl], axis=0))
        o_ref[:, pg * page:(pg + 1) * page] = jnp.sum(jnp.maximum(x, 0.0) * w, axis=0, keepdims=True)
    xs = jnp.sum(q * knew_ref[...], axis=1, keepdims=True)
    s_self = jnp.sum(jnp.maximum(xs, 0.0) * w, axis=0, keepdims=True)
    lane = lax.broadcasted_iota(I32, (1, LANES), 1)
    o_ref[:, n_pages * page:] = jnp.where(lane == 0, s_self, NEG_INF)


def _sample_scores(page_table, qi3, wi3, knew3, cache_kidx_t, scale):
    s, n_pages = page_table.shape
    _, idim, page = cache_kidx_t.shape
    h = qi3.shape[1]
    width = n_pages * page + LANES
    grid_spec = pltpu.PrefetchScalarGridSpec(
        num_scalar_prefetch=1,
        grid=(s,),
        in_specs=[pl.BlockSpec((None, h, idim), lambda i, pt: (i, 0, 0)),
                  pl.BlockSpec((None, h, 1), lambda i, pt: (i, 0, 0)),
                  pl.BlockSpec((None, 1, idim), lambda i, pt: (i, 0, 0)),
                  pl.BlockSpec(memory_space=pl.ANY)],
        out_specs=pl.BlockSpec((None, 1, width), lambda i, pt: (i, 0, 0)),
        scratch_shapes=[pltpu.VMEM((2, n_pages, idim, page), F32), pltpu.SemaphoreType.DMA((2,))],
    )
    return pl.pallas_call(
        functools.partial(_sscore_kernel, n_pages=n_pages, scale=scale),
        out_shape=_sds((s, 1, width), F32),
        grid_spec=grid_spec,
        compiler_params=_cp(("arbitrary",)),
        name="sample_scores",
    )(page_table, qi3, wi3, knew3, cache_kidx_t)


def _sselect_kernel(sc_ref, idx_ref, rank_s, *, kk, n_valid):
    s, width = sc_ref.shape
    nblk = width // LANES
    pos = lax.broadcasted_iota(I32, (s, width), 1)
    keys = jnp.where(pos < n_valid, _sortable(sc_ref[...] + 0.0), INT_MIN)
    cnt = lambda pred: jnp.sum(pred.astype(I32), axis=1, keepdims=True)
    thr = _kth_largest(lambda t: cnt(keys >= t), kk, s)
    gt = keys > thr
    eq = keys == thr
    need = (kk - cnt(gt)).astype(F32)
    tri = (lax.broadcasted_iota(I32, (LANES, LANES), 0) < lax.broadcasted_iota(I32, (LANES, LANES), 1)).astype(BF16)
    seen_eq = jnp.zeros((s, 1), F32)
    seen_sel = jnp.zeros((s, 1), F32)
    for b in range(nblk):
        sl = slice(b * LANES, (b + 1) * LANES)
        eqf = jnp.where(eq[:, sl], 1.0, 0.0)
        before_eq = seen_eq + _dot(eqf.astype(BF16), tri)
        sel = gt[:, sl] | (eq[:, sl] & (before_eq < need))
        self_f = jnp.where(sel, 1.0, 0.0)
        rank = seen_sel + _dot(self_f.astype(BF16), tri)
        rank_s[:, sl] = jnp.where(sel, rank, -1.0)
        seen_eq = seen_eq + jnp.sum(eqf, axis=1, keepdims=True)
        seen_sel = seen_sel + jnp.sum(self_f, axis=1, keepdims=True)
    jrow = lax.broadcasted_iota(I32, (width, LANES), 0)
    lcol = lax.broadcasted_iota(I32, (width, LANES), 1)
    parts = jnp.where(lcol == 0, jrow >> 7, jnp.where(lcol == 1, jrow & (LANES - 1), 0)).astype(F32).astype(BF16)
    r_iota = lax.broadcasted_iota(I32, (kk, width), 0).astype(F32)

    def body(q, carry):
        onehot = jnp.where(rank_s[pl.ds(q, 1), :] == r_iota, 1.0, 0.0).astype(BF16)
        res = _dot(onehot, parts)
        idx_ref[q] = (res[:, 0:1] * LANES + res[:, 1:2]).astype(I32)
        return carry
    lax.fori_loop(0, s, body, 0)


def _sample_select(scores, kk, n_valid):
    s, width = scores.shape
    return pl.pallas_call(
        functools.partial(_sselect_kernel, kk=kk, n_valid=n_valid),
        out_shape=_sds((s, kk, 1), I32),
        grid=(1,),
        in_specs=[pl.BlockSpec((s, width), lambda i: (0, 0))],
        out_specs=pl.BlockSpec((s, kk, 1), lambda i: (0, 0, 0)),
        scratch_shapes=[pltpu.VMEM((s, width), F32)],
        compiler_params=_cp(("arbitrary",)),
        name="sample_select",
    )(scores)


def _sattn_kernel(idx_ref, pt_ref, q_ref, idxc_ref, idxr_ref, rbh_ref, rbl_ref, knew_ref, vnew_ref, ck_ref, cv_ref, o_ref,
                  kbuf, vbuf, sem, *, kk, past_len, page, n_buckets):
    s = pl.program_id(0)
    ns = pl.num_programs(0)
    n_heads, dh = q_ref.shape
    page_bits = page.bit_length() - 1

    def issue(seq, slot):
        def body(r, carry):
            idx = jnp.minimum(idx_ref[seq, r], past_len - 1)
            phys = pt_ref[seq, lax.shift_right_logical(idx, page_bits)]
            off = idx & (page - 1)
            pltpu.make_async_copy(ck_ref.at[phys, off], kbuf.at[slot, :, r], sem.at[0, slot]).start()
            pltpu.make_async_copy(cv_ref.at[phys, off], vbuf.at[slot, :, r], sem.at[1, slot]).start()
            return carry
        lax.fori_loop(0, kk, body, 0, unroll=8)

    @pl.when(s == 0)
    def _():
        issue(0, 0)

    @pl.when(s + 1 < ns)
    def _():
        issue(s + 1, (s + 1) % 2)

    slot = s % 2
    pltpu.make_async_copy(kbuf.at[slot], kbuf.at[slot], sem.at[0, slot]).wait()
    pltpu.make_async_copy(vbuf.at[slot], vbuf.at[slot], sem.at[1, slot]).wait()

    idxr = idxr_ref[...]
    dist = past_len - idxr
    bkt = _bucket_jnp(dist, n_buckets)
    onehot = jnp.where(bkt == lax.broadcasted_iota(I32, (n_buckets, kk), 0), 1.0, 0.0).astype(BF16)
    bias = _dot(rbh_ref[...], onehot) + _dot(rbl_ref[...], onehot)
    is_new = idxc_ref[...] == past_len
    qb = q_ref[...].astype(BF16)
    rows = []
    for h in range(n_heads):
        kh = jnp.where(is_new, knew_ref[h:h + 1, :], kbuf[slot, h])
        rows.append(_dot_nt(qb, kh.astype(BF16))[h:h + 1, :])
    logit = jnp.concatenate(rows, axis=0) * (dh ** -0.5) + bias
    logit = jnp.where(dist >= 0, logit, NEG_INF)
    p = jnp.exp(logit - jnp.max(logit, axis=1, keepdims=True))
    pb = (p / jnp.sum(p, axis=1, keepdims=True)).astype(BF16)
    for h in range(n_heads):
        vh = jnp.where(is_new, vnew_ref[h:h + 1, :], vbuf[slot, h])
        o_ref[h:h + 1, :] = _dot(pb, vh.astype(BF16))[h:h + 1, :]


def _sample_attention(idx3, page_table, q3, rel_bias, knew3, vnew3, cache_k, cache_v, past_len):
    s, kk, _ = idx3.shape
    n_heads, dh = q3.shape[1:]
    page = cache_k.shape[1]
    assert page & (page - 1) == 0
    n_buckets = rel_bias.shape[0]
    rbt = rel_bias.T
    rbh = rbt.astype(BF16)
    rbl = (rbt - rbh.astype(F32)).astype(BF16)
    seq3 = lambda: pl.BlockSpec((None, n_heads, dh), lambda i, a, b: (i, 0, 0))
    grid_spec = pltpu.PrefetchScalarGridSpec(
        num_scalar_prefetch=2,
        grid=(s,),
        in_specs=[seq3(),
                  pl.BlockSpec((None, kk, 1), lambda i, a, b: (i, 0, 0)),
                  pl.BlockSpec((None, 1, kk), lambda i, a, b: (i, 0, 0)),
                  pl.BlockSpec((n_heads, n_buckets), lambda i, a, b: (0, 0)),
                  pl.BlockSpec((n_heads, n_buckets), lambda i, a, b: (0, 0)),
                  seq3(), seq3(),
                  pl.BlockSpec(memory_space=pl.ANY), pl.BlockSpec(memory_space=pl.ANY)],
        out_specs=seq3(),
        scratch_shapes=[pltpu.VMEM((2, n_heads, kk, dh), F32), pltpu.VMEM((2, n_heads, kk, dh), F32),
                        pltpu.SemaphoreType.DMA((2, 2))],
    )
    return pl.pallas_call(
        functools.partial(_sattn_kernel, kk=kk, past_len=past_len, page=page, n_buckets=n_buckets),
        out_shape=_sds((s, n_heads, dh), F32),
        grid_spec=grid_spec,
        compiler_params=_cp(("arbitrary",)),
        name="sample_attention",
    )(idx3.reshape(s, kk), page_table, q3, idx3, idx3.reshape(s, 1, kk), rbh, rbl, knew3, vnew3, cache_k, cache_v)


def kernel(x_prompt, x_sample, cache_k, cache_v, cache_kidx, page_table, state_lru_h, state_lru_conv, state_ffn_conv,
           c_prompt, c_sample, w_ada, b_ada, w_in, lru_conv_w, lru_conv_b, lru_w_a, lru_b_a, lru_w_x, lru_b_x,
           lru_lambda, attn_rel_bias, lru_out_g, attn_out_g, w_out, ln1_g, ln1_b, w_ffn_in, ffn_conv_w, ffn_conv_b,
           w_ffn_out, ln2_g, ln2_b):
    B, T, D = x_prompt.shape
    S, ts, _ = x_sample.shape
    assert ts == 1
    _, page, n_heads, dh = cache_k.shape
    idim = cache_kidx.shape[-1]
    n_pages = page_table.shape[1]
    past_len = n_pages * page
    W = lru_conv_b.shape[0]
    aw = n_heads * dh
    n_idx_heads = (w_in.shape[1] - 2 * W - 3 * aw - idim) // (idim + 1)
    nqi = n_idx_heads * idim
    dff = ffn_conv_b.shape[0]
    assert W % LANES == 0 and dh == LANES and 2 * idim == LANES and S % 8 == 0

    w_in_t = w_in.T
    c0 = 2 * W + 3 * aw
    w_idx = jnp.concatenate([w_in_t[c0:c0 + nqi], w_in_t[c0 + nqi + n_idx_heads:],
                             w_in_t[c0 + nqi:c0 + nqi + n_idx_heads],
                             jnp.zeros((LANES - idim - n_idx_heads, D), F32)], axis=0)
    lru_p = dict(cw=lru_conv_w, cb=lru_conv_b.reshape(1, W),
                 wax=jnp.concatenate([lru_w_a, lru_w_x], axis=2).astype(BF16),
                 ba=lru_b_a.reshape(1, W), bx=lru_b_x.reshape(1, W), lam=lru_lambda.reshape(1, W),
                 g=lru_out_g.reshape(1, W))
    ln1 = (ln1_g.reshape(1, D), ln1_b.reshape(1, D))
    ln2 = (ln2_g.reshape(1, D), ln2_b.reshape(1, D))
    fcw, fcb = ffn_conv_w, ffn_conv_b.reshape(1, dff)
    attn_g = attn_out_g.reshape(1, aw)

    mp = -(-(S + B) // 8) * 8
    c_all = jnp.concatenate([c_sample, c_prompt, jnp.zeros((mp - S - B, D), F32)], axis=0)
    mod = _ada(c_all, w_ada, b_ada)

    st_p = _Stream(B, T, min(1024, T), False, S)
    st_s = _Stream(1, S, S, True, 0)
    xs3 = x_sample.reshape(1, S, D)
    tm = st_p.tm
    MP = B * T

    qi_p, wk_p, m1_p, w_in_b = _idx_project(st_p, x_prompt, mod, w_idx, nqi, w_in_t, c0)
    qi_s, wk_s, m1_s = _idx_project(st_s, xs3, mod, w_idx, nqi)
    m1_p = m1_p.reshape(MP, D)
    m1_s = m1_s.reshape(S, D)
    xg_p, xg_s = _matmul_nt(m1_p, m1_s, w_in_b, 0, 2 * W, [F32], [F32], tm)
    q_p, q_s = _matmul_nt(m1_p, m1_s, w_in_b, 2 * W, aw, [BF16], [F32], tm)
    k_p, kb_p, k_s = _matmul_nt(m1_p, m1_s, w_in_b, 2 * W + aw, aw, [F32, BF16], [F32], tm)
    v_p, vb_p, v_s = _matmul_nt(m1_p, m1_s, w_in_b, 2 * W + 2 * aw, aw, [F32, BF16], [F32], tm)
    qi_p, wk_p = qi_p.reshape(MP, nqi), wk_p.reshape(MP, LANES)
    qi_s, wk_s = qi_s.reshape(S, nqi), wk_s.reshape(S, LANES)
    kidx_p, kidx_s = wk_p[:, :idim], wk_s[:, :idim]

    ylru_p, h_p, w_out_b = _lru_prompt(xg_p, B, T, jnp.zeros((B, 8, W), F32), jnp.zeros((B, 1, W), F32), lru_p, w_out)
    conv_p = xg_p.reshape(B, T, 2 * W)[:, T - 3:, :W]
    mask = _prompt_index(qi_p, wk_p, B, T, n_idx_heads, idim, min(TOPK_MAX, T // 4))
    yatt_p, w_ffn_in_b = _prompt_attention(q_p, kb_p, vb_p, mask, attn_rel_bias, B, T, n_heads, dh, w_ffn_in)

    ylru_s, h_s = _lru_sample(xg_s, jnp.swapaxes(state_lru_conv, 0, 1), state_lru_h, lru_p)
    conv_s = jnp.concatenate([state_lru_conv[:, 1:], xg_s[:, None, :W]], axis=1)
    scores = _sample_scores(page_table, qi_s.reshape(S, n_idx_heads, idim),
                            wk_s[:, idim:idim + n_idx_heads].reshape(S, n_idx_heads, 1),
                            kidx_s.reshape(S, 1, idim), jnp.swapaxes(cache_kidx, 1, 2),
                            (n_idx_heads ** -0.5) * (idim ** -0.5))
    idx3 = _sample_select(scores.reshape(S, past_len + LANES), min(TOPK_MAX, (past_len + 1) // 4), past_len + 1)
    yatt_s = _sample_attention(idx3, page_table, q_s.reshape(S, n_heads, dh), attn_rel_bias,
                               k_s.reshape(S, n_heads, dh), v_s.reshape(S, n_heads, dh),
                               cache_k, cache_v, past_len).reshape(S, aw)

    mix_p, mix_s = _matmul_cat(ylru_p, _rms_norm(yatt_p, attn_g, 512), ylru_s, _rms_norm(yatt_s, attn_g, 512),
                               w_out_b, tm)
    x1_p, m2_p = _residual_ln(st_p, x_prompt, mix_p.reshape(B, T, D), mod, 2, *ln1, mod_next=(3, 4))
    x1_s, m2_s = _residual_ln(st_s, xs3, mix_s.reshape(1, S, D), mod, 2, *ln1, mod_next=(3, 4))
    hmid_p, tail, w_ffn_out_b, hmid_s, gate_s = _ffn_in(
        m2_p.reshape(MP, D), m2_s.reshape(S, D), w_ffn_in_b, jnp.zeros((B, 8, dff), F32),
        jnp.swapaxes(state_ffn_conv, 0, 1), fcw, fcb, B, T, w_ffn_out)
    ffn_p = tail[:, 6:, :]
    ffn_s = jnp.concatenate([state_ffn_conv[:, 1:], gate_s[:, None, :]], axis=1)
    f_p, f_s = _matmul_ktiled(hmid_p, hmid_s, w_ffn_out_b, tm, 1024, 4096)
    y_p = _residual_ln(st_p, x1_p, f_p.reshape(B, T, D), mod, 5, *ln2)
    y_s = _residual_ln(st_s, x1_s, f_s.reshape(1, S, D), mod, 5, *ln2)

    return (y_p, y_s.reshape(S, 1, D),
            k_p.reshape(B, T, n_heads, dh), v_p.reshape(B, T, n_heads, dh), kidx_p.reshape(B, T, idim), h_p, conv_p,
            ffn_p,
            k_s.reshape(S, 1, n_heads, dh), v_s.reshape(S, 1, n_heads, dh), kidx_s.reshape(S, 1, idim), h_s, conv_s,
            ffn_s)
```
